```python
import jax
import jax.numpy as jnp
from jax import lax
import numpy as np

D_MODEL = 1024
BATCH = 4
SEQ = 8192
DEPTH = 2
DEC_BATCH = 16
DEC_SEQ = 2048
PAST_LEN = 128

GRID_W = 64
MLSTM_HEADS = 4
MLSTM_WIDTH = D_MODEL
MLSTM_HEAD_DIM = MLSTM_WIDTH // MLSTM_HEADS
MLSTM_CHUNK = 128
NA_HEADS = 16
NA_HEAD_DIM = 64
NA_WIDTH = NA_HEADS * NA_HEAD_DIM
NA_KH = 8
NA_KW = 16
LRU_WIDTH = D_MODEL
LRU_BLOCKS = 16
LRU_BLOCK = LRU_WIDTH // LRU_BLOCKS
LRU_C = 8.0
CONV_W = 4
N_BRANCH = 3
N_EXPERTS = 16
EC_FACTOR = 2
D_EXPERT = 2048
N_MOD = 6
EPS = 1e-6
NEG_INF = -1e30
IN_WIDTHS = (MLSTM_WIDTH, MLSTM_WIDTH, MLSTM_WIDTH, MLSTM_WIDTH, 4 * MLSTM_HEADS, NA_WIDTH, NA_WIDTH, NA_WIDTH, LRU_WIDTH, LRU_WIDTH, N_BRANCH * D_MODEL)
N_IN = sum(IN_WIDTHS)

kernel_name = 'hybrid_mlstm_natten_rglru_ec_encoder'


def _rmsnorm(x, w):
    xf = x.astype(jnp.float32)
    y = xf * lax.rsqrt(jnp.mean(xf * xf, axis=-1, keepdims=True) + EPS)
    return (y * w.astype(jnp.float32)).astype(x.dtype)


def _split_in(proj):
    points = []
    acc = 0
    for w in IN_WIDTHS[:-1]:
        acc += w
        points.append(acc)
    return jnp.split(proj, points, axis=-1)


def _mlstm_scan(q, k, v, ig, lf):
    B, H, T, dh = q.shape
    L = MLSTM_CHUNK
    nc = T // L

    def chunks(t):
        return jnp.moveaxis(t.reshape((B, H, nc, L) + t.shape[3:]), 2, 0)

    tril = jnp.tril(jnp.ones((L, L), dtype=bool))

    def step(carry, inp):
        C, n, m = carry
        qc, kc, vc, ic, fc = inp
        b = jnp.cumsum(fc, axis=-1)
        dmat = jnp.where(tril, b[..., :, None] - b[..., None, :] + ic[..., None, :], NEG_INF)
        g = b + m[..., None]
        m_row = jnp.maximum(g, jnp.max(dmat, axis=-1))
        w_intra = jnp.exp(dmat - m_row[..., None])
        w_inter = jnp.exp(g - m_row)
        s = jnp.einsum('bhid,bhjd->bhij', qc, kc) * w_intra
        num = jnp.einsum('bhij,bhje->bhie', s, vc) + w_inter[..., None] * jnp.einsum('bhid,bhde->bhie', qc, C)
        den = jnp.sum(s, axis=-1) + w_inter * jnp.einsum('bhid,bhd->bhi', qc, n)
        h = num / jnp.maximum(jnp.abs(den), jnp.exp(-m_row))[..., None]
        b_last = b[..., -1]
        w_last = b_last[..., None] - b + ic
        m_new = jnp.maximum(b_last + m, jnp.max(w_last, axis=-1))
        w_k = jnp.exp(w_last - m_new[..., None])
        decay = jnp.exp(b_last + m - m_new)
        C_new = decay[..., None, None] * C + jnp.einsum('bhj,bhjd,bhje->bhde', w_k, kc, vc)
        n_new = decay[..., None] * n + jnp.einsum('bhj,bhjd->bhd', w_k, kc)
        return (C_new, n_new, m_new), h

    init = (jnp.zeros((B, H, dh, dh), jnp.float32), jnp.zeros((B, H, dh), jnp.float32), jnp.zeros((B, H), jnp.float32))
    _, h = lax.scan(step, init, (chunks(q), chunks(k), chunks(v), chunks(ig), chunks(lf)))
    return jnp.moveaxis(h, 0, 2).reshape(B, H, T, dh)


def _mlstm_bidir(q, k, v, o, gates, norm_w):
    B, T, _ = q.shape
    H, dh = MLSTM_HEADS, MLSTM_HEAD_DIM

    def heads(t):
        return t.astype(jnp.float32).reshape(B, T, H, dh).transpose(0, 2, 1, 3)

    qh = heads(q)
    kh = heads(k) * (dh ** -0.5)
    vh = heads(v)
    gt = gates.astype(jnp.float32).reshape(B, T, 4, H).transpose(2, 0, 3, 1)
    h_fwd = _mlstm_scan(qh, kh, vh, gt[0], jax.nn.log_sigmoid(gt[1]))

    def flip(t):
        return jnp.flip(t, axis=2)

    h_bwd = flip(_mlstm_scan(flip(qh), flip(kh), flip(vh), flip(gt[2]), flip(jax.nn.log_sigmoid(gt[3]))))
    h = h_fwd + h_bwd
    mu = jnp.mean(h, axis=-1, keepdims=True)
    var = jnp.mean(jnp.square(h - mu), axis=-1, keepdims=True)
    h = (h - mu) * lax.rsqrt(var + EPS)
    h = h.transpose(0, 2, 1, 3).reshape(B, T, H * dh) * norm_w.astype(jnp.float32)
    return (jax.nn.sigmoid(o.astype(jnp.float32)) * h).astype(q.dtype)


def _neigh_attn(q, k, v, rpb):
    B, T, H, dh = q.shape
    rows = T // GRID_W
    kh = min(NA_KH, rows)
    qg = q.reshape(B, rows, GRID_W, H, dh) * (dh ** -0.5)
    kg = k.reshape(B, rows, GRID_W, H, dh)
    vg = v.reshape(B, rows, GRID_W, H, dh)
    cols = jnp.arange(GRID_W)
    cs = jnp.clip(cols - NA_KW // 2, 0, GRID_W - NA_KW)
    col_in = (cols[None, :] >= cs[:, None]) & (cols[None, :] < cs[:, None] + NA_KW)
    dc_idx = jnp.clip(cols[None, :] - cols[:, None], -(NA_KW - 1), NA_KW - 1) + NA_KW - 1

    def row_block(r):
        rs = jnp.clip(r - kh // 2, 0, rows - kh)
        kb = lax.dynamic_slice_in_dim(kg, rs, kh, axis=1).reshape(B, kh * GRID_W, H, dh)
        vb = lax.dynamic_slice_in_dim(vg, rs, kh, axis=1).reshape(B, kh * GRID_W, H, dh)
        qr = lax.dynamic_index_in_dim(qg, r, axis=1, keepdims=False)
        dr_idx = rs + jnp.arange(kh) - r + NA_KH - 1
        bias = rpb[:, dr_idx][:, :, dc_idx].astype(jnp.float32)
        bias = jnp.where(col_in[None, None], bias, NEG_INF)
        bias = bias.transpose(0, 2, 1, 3).reshape(H, GRID_W, kh * GRID_W)
        s = jnp.einsum('bqhd,bkhd->bhqk', qr, kb).astype(jnp.float32) + bias[None]
        p = jax.nn.softmax(s, axis=-1)
        return jnp.einsum('bhqk,bkhd->bqhd', p.astype(vb.dtype), vb)

    out = lax.map(row_block, jnp.arange(rows))
    return jnp.transpose(out, (1, 0, 2, 3, 4)).reshape(B, T, H * dh)


def _lin_combine(e1, e2):
    a1, b1 = e1
    a2, b2 = e2
    return (a1 * a2, a2 * b1 + b2)


def _rglru_branch(x, y, conv_w, conv_b, wa, ba, wx, bx, lam):
    B, T, W = x.shape
    xf = x.astype(jnp.float32)
    pad = CONV_W // 2
    xp = jnp.pad(xf, ((0, 0), (pad, CONV_W - 1 - pad), (0, 0)))
    xc = conv_b.astype(jnp.float32)
    for j in range(CONV_W):
        xc = xc + xp[:, j:j + T] * conv_w[j].astype(jnp.float32)
    xb = xc.reshape(B, T, LRU_BLOCKS, LRU_BLOCK)

    def direction(d, rev):
        r = jax.nn.sigmoid(jnp.einsum('btnk,nkj->btnj', xb, wa[d].astype(jnp.float32)).reshape(B, T, W) + ba[d])
        i = jax.nn.sigmoid(jnp.einsum('btnk,nkj->btnj', xb, wx[d].astype(jnp.float32)).reshape(B, T, W) + bx[d])
        log_a = -LRU_C * r * jax.nn.softplus(-lam[d].astype(jnp.float32))
        a = jnp.exp(log_a)
        bterm = jnp.sqrt(-jnp.expm1(2.0 * log_a)) * (i * xc)
        if rev:
            a = jnp.flip(a, axis=1)
            bterm = jnp.flip(bterm, axis=1)
        _, h = lax.associative_scan(_lin_combine, (a, bterm), axis=1)
        if rev:
            h = jnp.flip(h, axis=1)
        return h

    h = direction(0, False) + direction(1, True)
    return (h * jax.nn.gelu(y.astype(jnp.float32))).astype(x.dtype)


def _token_mixer(xm, w_in, b_in, mlstm_norm_w, na_rpb, conv_w, conv_b, lru_wa, lru_ba, lru_wx, lru_bx, lru_L, w_br_a, w_br_b, w_br_c, w_out):
    B, T, _ = xm.shape
    proj = jnp.einsum('btd,dn->btn', xm, w_in) + b_in
    aq, ak, av, ao, agates, bq, bk, bv, cx, cy, mg = _split_in(proj)
    h_a = _mlstm_bidir(aq, ak, av, ao, agates, mlstm_norm_w)
    h_b = _neigh_attn(bq.reshape(B, T, NA_HEADS, NA_HEAD_DIM), bk.reshape(B, T, NA_HEADS, NA_HEAD_DIM), bv.reshape(B, T, NA_HEADS, NA_HEAD_DIM), na_rpb)
    h_c = _rglru_branch(cx, cy, conv_w, conv_b, lru_wa, lru_ba, lru_wx, lru_bx, lru_L)
    g = jax.nn.sigmoid(mg.astype(jnp.float32)).reshape(B, T, N_BRANCH, D_MODEL).astype(xm.dtype)
    merged = g[:, :, 0] * (h_a @ w_br_a) + g[:, :, 1] * (h_b @ w_br_b) + g[:, :, 2] * (h_c @ w_br_c)
    return merged @ w_out


def _expert_choice(xm, w_router, b_router, w_g, w_u, w_d):
    B, T, D = xm.shape
    N = B * T
    xf = xm.reshape(N, D)
    cap = EC_FACTOR * N // N_EXPERTS
    logits = (xf @ w_router).astype(jnp.float32) + b_router.astype(jnp.float32)
    aff = jax.nn.softmax(logits, axis=-1)
    gate, idx = lax.top_k(aff.T, cap)
    xe = jnp.take(xf, idx, axis=0)
    hid = jax.nn.silu(jnp.einsum('ecd,edf->ecf', xe, w_g)) * jnp.einsum('ecd,edf->ecf', xe, w_u)
    ye = jnp.einsum('ecf,efd->ecd', hid, w_d) * gate[..., None].astype(xm.dtype)
    out = jnp.zeros_like(xf).at[idx.reshape(-1)].add(ye.reshape(-1, D))
    return out.reshape(B, T, D)


def _trunk(x, c, norm1_w, norm2_w, w_mod, b_mod, w_in, b_in, mlstm_norm_w, na_rpb, conv_w, conv_b, lru_wa, lru_ba, lru_wx, lru_bx, lru_L, w_br_a, w_br_b, w_br_c, w_out, w_router, b_router, w_gate_e, w_up_e, w_down_e, final_norm_w):
    c_act = jax.nn.silu(c)
    for l in range(DEPTH):
        mod = (c_act @ w_mod[l] + b_mod[l])[:, None, :]
        sh1, sc1, g1, sh2, sc2, g2 = jnp.split(mod, N_MOD, axis=-1)
        xm = _rmsnorm(x, norm1_w[l]) * (1.0 + sc1) + sh1
        x = x + g1 * _token_mixer(xm, w_in[l], b_in[l], mlstm_norm_w[l], na_rpb[l], conv_w[l], conv_b[l], lru_wa[l], lru_ba[l], lru_wx[l], lru_bx[l], lru_L[l], w_br_a[l], w_br_b[l], w_br_c[l], w_out[l])
        xm = _rmsnorm(x, norm2_w[l]) * (1.0 + sc2) + sh2
        x = x + g2 * _expert_choice(xm, w_router[l], b_router[l], w_gate_e[l], w_up_e[l], w_down_e[l])
    return _rmsnorm(x, final_norm_w)


def setup_inputs(seed: int = 0) -> dict:
    key = jax.random.key(seed)
    ks = jax.random.split(key, 32)
    f32 = jnp.float32

    def nrm(k, shape, scale):
        return jax.random.normal(k, shape, f32) * scale

    x_prompt = nrm(ks[0], (BATCH, SEQ, D_MODEL), 1.0)
    x_sample = nrm(ks[1], (DEC_BATCH, DEC_SEQ, D_MODEL), 1.0)
    c_prompt = nrm(ks[2], (BATCH, D_MODEL), 1.0)
    c_sample = nrm(ks[3], (DEC_BATCH, D_MODEL), 1.0)
    norm1_w = 1.0 + nrm(ks[4], (DEPTH, D_MODEL), 0.02)
    norm2_w = 1.0 + nrm(ks[5], (DEPTH, D_MODEL), 0.02)
    w_mod = nrm(ks[6], (DEPTH, D_MODEL, N_MOD * D_MODEL), 0.5 * D_MODEL ** -0.5)
    b_mod = nrm(ks[7], (DEPTH, N_MOD * D_MODEL), 0.02)
    w_in = nrm(ks[8], (DEPTH, D_MODEL, N_IN), D_MODEL ** -0.5)
    b_in = nrm(ks[9], (DEPTH, N_IN), 0.02)
    off = 4 * MLSTM_WIDTH
    f_bias = jnp.linspace(3.0, 6.0, MLSTM_HEADS, dtype=f32)
    b_in = b_in.at[:, off + MLSTM_HEADS:off + 2 * MLSTM_HEADS].add(f_bias)
    b_in = b_in.at[:, off + 3 * MLSTM_HEADS:off + 4 * MLSTM_HEADS].add(f_bias)
    mlstm_norm_w = 1.0 + nrm(ks[10], (DEPTH, MLSTM_WIDTH), 0.02)
    na_rpb = nrm(ks[11], (DEPTH, NA_HEADS, 2 * NA_KH - 1, 2 * NA_KW - 1), 0.1)
    conv_w = nrm(ks[12], (DEPTH, CONV_W, LRU_WIDTH), CONV_W ** -0.5)
    conv_b = nrm(ks[13], (DEPTH, LRU_WIDTH), 0.02)
    lru_wa = nrm(ks[14], (DEPTH, 2, LRU_BLOCKS, LRU_BLOCK, LRU_BLOCK), LRU_BLOCK ** -0.5)
    lru_ba = nrm(ks[15], (DEPTH, 2, LRU_WIDTH), 0.02)
    lru_wx = nrm(ks[16], (DEPTH, 2, LRU_BLOCKS, LRU_BLOCK, LRU_BLOCK), LRU_BLOCK ** -0.5)
    lru_bx = nrm(ks[17], (DEPTH, 2, LRU_WIDTH), 0.02)
    u = jax.random.uniform(ks[18], (DEPTH, 2, LRU_WIDTH), f32, minval=0.9, maxval=0.999)
    a0 = u ** (1.0 / LRU_C)
    lru_L = jnp.log(a0) - jnp.log1p(-a0)
    w_br_a = nrm(ks[19], (DEPTH, MLSTM_WIDTH, D_MODEL), MLSTM_WIDTH ** -0.5)
    w_br_b = nrm(ks[20], (DEPTH, NA_WIDTH, D_MODEL), NA_WIDTH ** -0.5)
    w_br_c = nrm(ks[21], (DEPTH, LRU_WIDTH, D_MODEL), LRU_WIDTH ** -0.5)
    w_out = nrm(ks[22], (DEPTH, D_MODEL, D_MODEL), D_MODEL ** -0.5)
    w_router = nrm(ks[23], (DEPTH, D_MODEL, N_EXPERTS), D_MODEL ** -0.5)
    b_router = nrm(ks[24], (DEPTH, N_EXPERTS), 0.01)
    w_gate_e = nrm(ks[25], (DEPTH, N_EXPERTS, D_MODEL, D_EXPERT), D_MODEL ** -0.5)
    w_up_e = nrm(ks[26], (DEPTH, N_EXPERTS, D_MODEL, D_EXPERT), D_MODEL ** -0.5)
    w_down_e = nrm(ks[27], (DEPTH, N_EXPERTS, D_EXPERT, D_MODEL), D_EXPERT ** -0.5)
    final_norm_w = 1.0 + nrm(ks[28], (D_MODEL,), 0.02)
    return {'x_prompt': x_prompt, 'x_sample': x_sample, 'c_prompt': c_prompt, 'c_sample': c_sample,
            'norm1_w': norm1_w, 'norm2_w': norm2_w, 'w_mod': w_mod, 'b_mod': b_mod, 'w_in': w_in, 'b_in': b_in,
            'mlstm_norm_w': mlstm_norm_w, 'na_rpb': na_rpb, 'conv_w': conv_w, 'conv_b': conv_b,
            'lru_wa': lru_wa, 'lru_ba': lru_ba, 'lru_wx': lru_wx, 'lru_bx': lru_bx, 'lru_L': lru_L,
            'w_br_a': w_br_a, 'w_br_b': w_br_b, 'w_br_c': w_br_c, 'w_out': w_out,
            'w_router': w_router, 'b_router': b_router, 'w_gate_e': w_gate_e, 'w_up_e': w_up_e, 'w_down_e': w_down_e,
            'final_norm_w': final_norm_w}


def reference(x_prompt, x_sample, c_prompt, c_sample, norm1_w, norm2_w, w_mod, b_mod, w_in, b_in, mlstm_norm_w, na_rpb, conv_w, conv_b, lru_wa, lru_ba, lru_wx, lru_bx, lru_L, w_br_a, w_br_b, w_br_c, w_out, w_router, b_router, w_gate_e, w_up_e, w_down_e, final_norm_w):
    y_prompt = _trunk(x_prompt, c_prompt, norm1_w, norm2_w, w_mod, b_mod, w_in, b_in, mlstm_norm_w, na_rpb, conv_w, conv_b, lru_wa, lru_ba, lru_wx, lru_bx, lru_L, w_br_a, w_br_b, w_br_c, w_out, w_router, b_router, w_gate_e, w_up_e, w_down_e, final_norm_w)
    y_sample = _trunk(x_sample, c_sample, norm1_w, norm2_w, w_mod, b_mod, w_in, b_in, mlstm_norm_w, na_rpb, conv_w, conv_b, lru_wa, lru_ba, lru_wx, lru_bx, lru_L, w_br_a, w_br_b, w_br_c, w_out, w_router, b_router, w_gate_e, w_up_e, w_down_e, final_norm_w)
    return (y_prompt, y_sample)
```

```python
import functools

import jax
import jax.numpy as jnp
from jax import lax
from jax.experimental import pallas as pl
from jax.experimental.pallas import tpu as pltpu

EPS = 1e-6
NEG_INF = -1e30
GRID_W = 64
MLSTM_HEADS = 4
MLSTM_CHUNK = 128
NA_HEADS = 16
NA_KH = 8
NA_KW = 16
LRU_BLOCKS = 16
LRU_C = 8.0
CONV_W = 4
N_BRANCH = 3
N_EXPERTS = 16
EC_FACTOR = 2
N_MOD = 6


def _rmsnorm_body(x_ref, w_ref, o_ref):
    x = x_ref[...]
    y = x * lax.rsqrt(jnp.mean(x * x, axis=-1, keepdims=True) + EPS)
    o_ref[...] = y * w_ref[...]


def _rmsnorm_pallas(x2d, w):
    n, d = x2d.shape
    tm = 1024
    return pl.pallas_call(
        _rmsnorm_body,
        grid=(n // tm,),
        in_specs=[pl.BlockSpec((tm, d), lambda i: (i, 0)), pl.BlockSpec((1, d), lambda i: (0, 0))],
        out_specs=pl.BlockSpec((tm, d), lambda i: (i, 0)),
        out_shape=jax.ShapeDtypeStruct((n, d), jnp.float32),
        name="final_rmsnorm",
    )(x2d, w.reshape(1, d))


def _rmsnorm(x, w):
    return x * lax.rsqrt(jnp.mean(x * x, axis=-1, keepdims=True) + EPS) * w


def _mlstm_scan(q, k, v, ig, lf):
    B, H, T, dh = q.shape
    L = MLSTM_CHUNK
    nc = T // L

    def chunks(t):
        return jnp.moveaxis(t.reshape((B, H, nc, L) + t.shape[3:]), 2, 0)

    tril = jnp.tril(jnp.ones((L, L), dtype=bool))

    def step(carry, inp):
        C, n, m = carry
        qc, kc, vc, ic, fc = inp
        b = jnp.cumsum(fc, axis=-1)
        dmat = jnp.where(tril, b[..., :, None] - b[..., None, :] + ic[..., None, :], NEG_INF)
        g = b + m[..., None]
        m_row = jnp.maximum(g, jnp.max(dmat, axis=-1))
        w_intra = jnp.exp(dmat - m_row[..., None])
        w_inter = jnp.exp(g - m_row)
        s = jnp.einsum('bhid,bhjd->bhij', qc, kc) * w_intra
        num = jnp.einsum('bhij,bhje->bhie', s, vc) + w_inter[..., None] * jnp.einsum('bhid,bhde->bhie', qc, C)
        den = jnp.sum(s, axis=-1) + w_inter * jnp.einsum('bhid,bhd->bhi', qc, n)
        h = num / jnp.maximum(jnp.abs(den), jnp.exp(-m_row))[..., None]
        b_last = b[..., -1]
        w_last = b_last[..., None] - b + ic
        m_new = jnp.maximum(b_last + m, jnp.max(w_last, axis=-1))
        w_k = jnp.exp(w_last - m_new[..., None])
        decay = jnp.exp(b_last + m - m_new)
        C_new = decay[..., None, None] * C + jnp.einsum('bhj,bhjd,bhje->bhde', w_k, kc, vc)
        n_new = decay[..., None] * n + jnp.einsum('bhj,bhjd->bhd', w_k, kc)
        return (C_new, n_new, m_new), h

    init = (jnp.zeros((B, H, dh, dh), jnp.float32), jnp.zeros((B, H, dh), jnp.float32), jnp.zeros((B, H), jnp.float32))
    _, h = lax.scan(step, init, (chunks(q), chunks(k), chunks(v), chunks(ig), chunks(lf)))
    return jnp.moveaxis(h, 0, 2).reshape(B, H, T, dh)


def _mlstm_bidir(q, k, v, o, gates, norm_w):
    B, T, W = q.shape
    H = MLSTM_HEADS
    dh = W // H

    def heads(t):
        return t.reshape(B, T, H, dh).transpose(0, 2, 1, 3)

    qh, kh, vh = heads(q), heads(k) * (dh ** -0.5), heads(v)
    gt = gates.reshape(B, T, 4, H).transpose(2, 0, 3, 1)
    h_fwd = _mlstm_scan(qh, kh, vh, gt[0], jax.nn.log_sigmoid(gt[1]))
    flip = lambda t: jnp.flip(t, axis=2)
    h_bwd = flip(_mlstm_scan(flip(qh), flip(kh), flip(vh), flip(gt[2]), flip(jax.nn.log_sigmoid(gt[3]))))
    h = h_fwd + h_bwd
    mu = jnp.mean(h, axis=-1, keepdims=True)
    var = jnp.mean(jnp.square(h - mu), axis=-1, keepdims=True)
    h = (h - mu) * lax.rsqrt(var + EPS)
    h = h.transpose(0, 2, 1, 3).reshape(B, T, H * dh) * norm_w
    return jax.nn.sigmoid(o) * h


def _neigh_attn(q, k, v, rpb):
    B, T, H, dh = q.shape
    rows = T // GRID_W
    kh = min(NA_KH, rows)
    qg = q.reshape(B, rows, GRID_W, H, dh) * (dh ** -0.5)
    kg = k.reshape(B, rows, GRID_W, H, dh)
    vg = v.reshape(B, rows, GRID_W, H, dh)
    cols = jnp.arange(GRID_W)
    cs = jnp.clip(cols - NA_KW // 2, 0, GRID_W - NA_KW)
    col_in = (cols[None, :] >= cs[:, None]) & (cols[None, :] < cs[:, None] + NA_KW)
    dc_idx = jnp.clip(cols[None, :] - cols[:, None], -(NA_KW - 1), NA_KW - 1) + NA_KW - 1

    def row_block(r):
        rs = jnp.clip(r - kh // 2, 0, rows - kh)
        kb = lax.dynamic_slice_in_dim(kg, rs, kh, axis=1).reshape(B, kh * GRID_W, H, dh)
        vb = lax.dynamic_slice_in_dim(vg, rs, kh, axis=1).reshape(B, kh * GRID_W, H, dh)
        qr = lax.dynamic_index_in_dim(qg, r, axis=1, keepdims=False)
        dr_idx = rs + jnp.arange(kh) - r + NA_KH - 1
        bias = rpb[:, dr_idx][:, :, dc_idx]
        bias = jnp.where(col_in[None, None], bias, NEG_INF)
        bias = bias.transpose(0, 2, 1, 3).reshape(H, GRID_W, kh * GRID_W)
        s = jnp.einsum('bqhd,bkhd->bhqk', qr, kb) + bias[None]
        p = jax.nn.softmax(s, axis=-1)
        return jnp.einsum('bhqk,bkhd->bqhd', p, vb)

    out = lax.map(row_block, jnp.arange(rows))
    return jnp.transpose(out, (1, 0, 2, 3, 4)).reshape(B, T, H * dh)


def _rglru_branch(x, y, conv_w, conv_b, wa, ba, wx, bx, lam):
    B, T, W = x.shape
    nb = LRU_BLOCKS
    pad = CONV_W // 2
    xp = jnp.pad(x, ((0, 0), (pad, CONV_W - 1 - pad), (0, 0)))
    xc = conv_b
    for j in range(CONV_W):
        xc = xc + xp[:, j:j + T] * conv_w[j]
    xb = xc.reshape(B, T, nb, W // nb)

    def combine(e1, e2):
        a1, b1 = e1
        a2, b2 = e2
        return (a1 * a2, a2 * b1 + b2)

    def direction(d, rev):
        r = jax.nn.sigmoid(jnp.einsum('btnk,nkj->btnj', xb, wa[d]).reshape(B, T, W) + ba[d])
        i = jax.nn.sigmoid(jnp.einsum('btnk,nkj->btnj', xb, wx[d]).reshape(B, T, W) + bx[d])
        log_a = -LRU_C * r * jax.nn.softplus(-lam[d])
        a = jnp.exp(log_a)
        bterm = jnp.sqrt(-jnp.expm1(2.0 * log_a)) * (i * xc)
        if rev:
            a, bterm = jnp.flip(a, axis=1), jnp.flip(bterm, axis=1)
        _, h = lax.associative_scan(combine, (a, bterm), axis=1)
        return jnp.flip(h, axis=1) if rev else h

    return (direction(0, False) + direction(1, True)) * jax.nn.gelu(y)


def _token_mixer(xm, p, l):
    B, T, D = xm.shape
    proj = jnp.einsum('btd,dn->btn', xm, p['w_in'][l]) + p['b_in'][l]
    widths = (D, D, D, D, 4 * MLSTM_HEADS, D, D, D, D, D, N_BRANCH * D)
    points, acc = [], 0
    for w in widths[:-1]:
        acc += w
        points.append(acc)
    aq, ak, av, ao, agates, bq, bk, bv, cx, cy, mg = jnp.split(proj, points, axis=-1)
    h_a = _mlstm_bidir(aq, ak, av, ao, agates, p['mlstm_norm_w'][l])
    dh = D // NA_HEADS
    h_b = _neigh_attn(bq.reshape(B, T, NA_HEADS, dh), bk.reshape(B, T, NA_HEADS, dh), bv.reshape(B, T, NA_HEADS, dh), p['na_rpb'][l])
    h_c = _rglru_branch(cx, cy, p['conv_w'][l], p['conv_b'][l], p['lru_wa'][l], p['lru_ba'][l], p['lru_wx'][l], p['lru_bx'][l], p['lru_L'][l])
    g = jax.nn.sigmoid(mg).reshape(B, T, N_BRANCH, D)
    merged = g[:, :, 0] * (h_a @ p['w_br_a'][l]) + g[:, :, 1] * (h_b @ p['w_br_b'][l]) + g[:, :, 2] * (h_c @ p['w_br_c'][l])
    return merged @ p['w_out'][l]


def _expert_choice(xm, w_router, b_router, w_g, w_u, w_d):
    B, T, D = xm.shape
    N = B * T
    xf = xm.reshape(N, D)
    cap = EC_FACTOR * N // N_EXPERTS
    aff = jax.nn.softmax(xf @ w_router + b_router, axis=-1)
    gate, idx = lax.top_k(aff.T, cap)
    xe = jnp.take(xf, idx, axis=0)
    hid = jax.nn.silu(jnp.einsum('ecd,edf->ecf', xe, w_g)) * jnp.einsum('ecd,edf->ecf', xe, w_u)
    ye = jnp.einsum('ecf,efd->ecd', hid, w_d) * gate[..., None]
    out = jnp.zeros_like(xf).at[idx.reshape(-1)].add(ye.reshape(-1, D))
    return out.reshape(B, T, D)


def _trunk(x, c, p):
    depth = p['w_in'].shape[0]
    c_act = jax.nn.silu(c)
    for l in range(depth):
        mod = (c_act @ p['w_mod'][l] + p['b_mod'][l])[:, None, :]
        sh1, sc1, g1, sh2, sc2, g2 = jnp.split(mod, N_MOD, axis=-1)
        xm = _rmsnorm(x, p['norm1_w'][l]) * (1.0 + sc1) + sh1
        x = x + g1 * _token_mixer(xm, p, l)
        xm = _rmsnorm(x, p['norm2_w'][l]) * (1.0 + sc2) + sh2
        x = x + g2 * _expert_choice(xm, p['w_router'][l], p['b_router'][l], p['w_gate_e'][l], p['w_up_e'][l], p['w_down_e'][l])
    B, T, D = x.shape
    return _rmsnorm_pallas(x.reshape(B * T, D), p['final_norm_w']).reshape(B, T, D)


def kernel(x_prompt, x_sample, c_prompt, c_sample, norm1_w, norm2_w, w_mod, b_mod, w_in, b_in, mlstm_norm_w, na_rpb, conv_w, conv_b, lru_wa, lru_ba, lru_wx, lru_bx, lru_L, w_br_a, w_br_b, w_br_c, w_out, w_router, b_router, w_gate_e, w_up_e, w_down_e, final_norm_w):
    p = dict(norm1_w=norm1_w, norm2_w=norm2_w, w_mod=w_mod, b_mod=b_mod, w_in=w_in, b_in=b_in, mlstm_norm_w=mlstm_norm_w,
             na_rpb=na_rpb, conv_w=conv_w, conv_b=conv_b, lru_wa=lru_wa, lru_ba=lru_ba, lru_wx=lru_wx, lru_bx=lru_bx,
             lru_L=lru_L, w_br_a=w_br_a, w_br_b=w_br_b, w_br_c=w_br_c, w_out=w_out, w_router=w_router, b_router=b_router,
             w_gate_e=w_gate_e, w_up_e=w_up_e, w_down_e=w_down_e, final_norm_w=final_norm_w)
    return (_trunk(x_prompt, c_prompt, p), _trunk(x_sample, c_sample, p))
```

```python
import functools
from typing import NamedTuple

import jax
import jax.numpy as jnp
from jax import lax
from jax.experimental import pallas as pl
from jax.experimental.pallas import tpu as pltpu

EPS = 1e-6
NEG_INF = -1e30
GRID_W = 64
MLSTM_HEADS = 4
MLSTM_CHUNK = 128
NA_HEADS = 16
NA_KH = 8
NA_KW = 16
LRU_BLOCKS = 16
LRU_C = 8.0
CONV_W = 4
N_BRANCH = 3
N_EXPERTS = 16
EC_FACTOR = 2
N_MOD = 6

LANES = 128
SUBLANES = 8
MXU_DIM = 256
VMEM_LIMIT = 56 * 1024 * 1024

F32 = jnp.float32
BF16 = jnp.bfloat16


class Layout(NamedTuple):
    bp: int
    tp: int
    bs: int
    ts: int
    d: int

    @property
    def n_p(self):
        return self.bp * self.tp

    @property
    def n_s(self):
        return self.bs * self.ts

    @property
    def nt(self):
        return self.n_p + self.n_s

    def seq_index(self, tok0):
        return jnp.where(tok0 < self.n_p, tok0 // self.tp, self.bp + (tok0 - self.n_p) // self.ts)

    def seq_len(self, tok0):
        return jnp.where(tok0 < self.n_p, self.tp, self.ts)

    def pos_in_seq(self, tok0):
        return jnp.where(tok0 < self.n_p, tok0 % self.tp, (tok0 - self.n_p) % self.ts)


def _cparams(*sem):
    return pltpu.CompilerParams(dimension_semantics=sem, vmem_limit_bytes=VMEM_LIMIT)


def _dot(a, b):
    return jnp.dot(a, b, preferred_element_type=F32)


def _dot_nt(a, b):
    return lax.dot_general(a, b, (((1,), (1,)), ((), ())), preferred_element_type=F32)


def _dot_tn(a, b):
    return lax.dot_general(a, b, (((0,), (0,)), ((), ())), preferred_element_type=F32)


def _sigmoid(x):
    return 1.0 / (1.0 + jnp.exp(-x))


def _rms(x, w):
    return x * lax.rsqrt(jnp.mean(x * x, axis=-1, keepdims=True) + EPS) * w


def _mod_body(c_ref, w_ref, b_ref, o_ref):
    c = c_ref[...]
    ca = (c * _sigmoid(c)).astype(BF16)
    o_ref[0] = _dot(ca, w_ref[0].astype(BF16)) + b_ref[0]


def _modulation(c_all, w_mod, b_mod):
    depth, d, nm = w_mod.shape
    rows = c_all.shape[0]
    tn = 1536
    return pl.pallas_call(
        _mod_body,
        grid=(depth, nm // tn),
        in_specs=[pl.BlockSpec((rows, d), lambda l, j: (0, 0)),
                  pl.BlockSpec((1, d, tn), lambda l, j: (l, 0, j)),
                  pl.BlockSpec((1, 1, tn), lambda l, j: (l, 0, j))],
        out_specs=pl.BlockSpec((1, rows, tn), lambda l, j: (l, 0, j)),
        out_shape=jax.ShapeDtypeStruct((depth, rows, nm), F32),
        compiler_params=_cparams("arbitrary", "arbitrary"),
        name="adaln_modulation",
    )(c_all, w_mod, b_mod.reshape(depth, 1, nm))


def _inproj_body(x_ref, mod_ref, nw_ref, wb_ref, bb_ref, wf_ref, bf_ref, wg_ref, bg_ref,
                 ob_ref, of_ref, og_ref, xm_ref, *, n_bf):
    j = pl.program_id(1)

    @pl.when(j == 0)
    def _():
        m = mod_ref[0]
        xm = (_rms(x_ref[...], nw_ref[...]) * (1.0 + m[1:2]) + m[0:1]).astype(BF16)
        xm_ref[...] = xm
        og_ref[...] = _dot(xm, wg_ref[...]) + bg_ref[...]

    @pl.when(j < n_bf)
    def _():
        ob_ref[...] = (_dot(xm_ref[...], wb_ref[...]) + bb_ref[...]).astype(BF16)

    @pl.when(j >= n_bf)
    def _():
        of_ref[...] = _dot(xm_ref[...], wf_ref[...]) + bf_ref[...]


def _input_projection(lay, x, mod, norm_w, wb, bb, wf, bf, wg, bg, tm):
    nt, d = x.shape
    n_bf, n_f = wb.shape[1] // d, wf.shape[1] // d
    seq = lambda i: lay.seq_index(i * tm)
    return pl.pallas_call(
        functools.partial(_inproj_body, n_bf=n_bf),
        grid=(nt // tm, n_bf + n_f),
        in_specs=[pl.BlockSpec((tm, d), lambda i, j: (i, 0)),
                  pl.BlockSpec((1, N_MOD, d), lambda i, j: (seq(i), 0, 0)),
                  pl.BlockSpec((1, d), lambda i, j: (0, 0)),
                  pl.BlockSpec((d, d), lambda i, j: (0, jnp.minimum(j, n_bf - 1))),
                  pl.BlockSpec((1, d), lambda i, j: (0, jnp.minimum(j, n_bf - 1))),
                  pl.BlockSpec((d, d), lambda i, j: (0, jnp.maximum(j - n_bf, 0))),
                  pl.BlockSpec((1, d), lambda i, j: (0, jnp.maximum(j - n_bf, 0))),
                  pl.BlockSpec((d, LANES), lambda i, j: (0, 0)),
                  pl.BlockSpec((1, LANES), lambda i, j: (0, 0))],
        out_specs=[pl.BlockSpec((tm, d), lambda i, j: (i, jnp.minimum(j, n_bf - 1))),
                   pl.BlockSpec((tm, d), lambda i, j: (i, jnp.maximum(j - n_bf, 0))),
                   pl.BlockSpec((tm, LANES), lambda i, j: (i, 0))],
        out_shape=[jax.ShapeDtypeStruct((nt, n_bf * d), BF16),
                   jax.ShapeDtypeStruct((nt, n_f * d), F32),
                   jax.ShapeDtypeStruct((nt, LANES), F32)],
        scratch_shapes=[pltpu.VMEM((tm, d), BF16)],
        compiler_params=_cparams("arbitrary", "arbitrary"),
        name="norm_inproj",
    )(x, mod, norm_w, wb, bb, wf, bf, wg, bg)


def _lane_cumsum(x, rev):
    n = x.shape[-1]
    lane = lax.broadcasted_iota(jnp.int32, x.shape, 1)
    k = 1
    while k < n:
        if rev:
            x = x + jnp.where(lane < n - k, pltpu.roll(x, n - k, 1), 0.0)
        else:
            x = x + jnp.where(lane >= k, pltpu.roll(x, k, 1), 0.0)
        k *= 2
    return x


def _log_sigmoid(x):
    return jnp.minimum(x, 0.0) - jnp.log(1.0 + jnp.exp(-jnp.abs(x)))


def _mlstm_body(*refs, lay, rev, heads, chunk):
    if rev:
        q_ref, k_ref, v_ref, gr_ref, gc_ref, hf_ref, o_ref, nw_ref, out_ref, c_ref, n_ref, m_ref = refs
    else:
        q_ref, k_ref, v_ref, gr_ref, gc_ref, out_ref, c_ref, n_ref, m_ref = refs
    s = pl.program_id(0)
    ci = (pl.num_programs(0) - 1 - s) if rev else s
    tok0 = ci * chunk
    pos = lay.pos_in_seq(tok0)
    first = (pos + chunk == lay.seq_len(tok0)) if rev else (pos == 0)

    @pl.when(first)
    def _():
        c_ref[...] = jnp.zeros_like(c_ref)
        n_ref[...] = jnp.zeros_like(n_ref)
        m_ref[...] = jnp.zeros_like(m_ref)

    dh = q_ref.shape[1] // heads
    gr = gr_ref[...]
    gc = gc_ref[...]
    gi = 2 * heads if rev else 0
    lf_all = _log_sigmoid(gr)
    b_all = _lane_cumsum(lf_all, rev)
    ri = lax.broadcasted_iota(jnp.int32, (chunk, chunk), 0)
    cj = lax.broadcasted_iota(jnp.int32, (chunk, chunk), 1)
    mask = (cj >= ri) if rev else (cj <= ri)
    kscale = dh ** -0.5
    for h in range(heads):
        sl = slice(h * dh, (h + 1) * dh)
        q = q_ref[:, sl]
        k = (k_ref[:, sl].astype(F32) * kscale).astype(BF16)
        v = v_ref[:, sl]
        ig_row = gr[gi + h:gi + h + 1, :]
        lf_row = lf_all[gi + heads + h:gi + heads + h + 1, :]
        b_row = b_all[gi + heads + h:gi + heads + h + 1, :]
        ig_col = gc[:, gi + h:gi + h + 1]
        m_old = m_ref[h][:, 0:1]
        b_col = jnp.sum(jnp.where(mask, lf_row, 0.0), axis=1, keepdims=True)
        dmat = jnp.where(mask, b_col - b_row + ig_row, NEG_INF)
        g = b_col + m_old
        m_row = jnp.maximum(g, jnp.max(dmat, axis=1, keepdims=True))
        w_intra = jnp.exp(dmat - m_row)
        w_inter = jnp.exp(g - m_row)
        sc = _dot_nt(q, k) * w_intra
        c_old = c_ref[h]
        n_old = n_ref[h]
        qf = q.astype(F32)
        num = _dot(sc.astype(BF16), v) + w_inter * _dot(q, c_old.astype(BF16))
        den = jnp.sum(sc, axis=1, keepdims=True) + w_inter * jnp.sum(qf * n_old, axis=1, keepdims=True)
        hh = num / jnp.maximum(jnp.abs(den), jnp.exp(-m_row))
        b_last = b_row[:, 0:1] if rev else b_row[:, chunk - 1:chunk]
        w_last_row = b_last - b_row + ig_row
        m_new = jnp.maximum(b_last + m_old, jnp.max(w_last_row, axis=1, keepdims=True))
        w_k = jnp.exp(b_last - b_col + ig_col - m_new)
        decay = jnp.exp(b_last + m_old - m_new)
        kw = k.astype(F32) * w_k
        c_ref[h] = decay * c_old + _dot_tn(kw.astype(BF16), v)
        n_ref[h] = decay * n_old + jnp.sum(kw, axis=0, keepdims=True)
        m_ref[h] = jnp.broadcast_to(m_new, m_ref.shape[1:])
        if rev:
            hs = hf_ref[:, sl] + hh
            mu = jnp.mean(hs, axis=1, keepdims=True)
            var = jnp.mean(jnp.square(hs - mu), axis=1, keepdims=True)
            hn = (hs - mu) * lax.rsqrt(var + EPS) * nw_ref[:, sl]
            out_ref[:, sl] = (_sigmoid(o_ref[:, sl]) * hn).astype(out_ref.dtype)
        else:
            out_ref[:, sl] = hh


def _mlstm_pass(lay, proj_b, proj_f, gates, gates_t, norm_w, h_fwd, rev):
    nt, d = lay.nt, lay.d
    chunk, heads = MLSTM_CHUNK, MLSTM_HEADS
    nc = nt // chunk
    cidx = (lambda s: nc - 1 - s) if rev else (lambda s: s)
    in_specs = [pl.BlockSpec((chunk, d), lambda s: (cidx(s), 0)),
                pl.BlockSpec((chunk, d), lambda s: (cidx(s), 1)),
                pl.BlockSpec((chunk, d), lambda s: (cidx(s), 2)),
                pl.BlockSpec((4 * heads, chunk), lambda s: (0, cidx(s))),
                pl.BlockSpec((chunk, LANES), lambda s: (cidx(s), 0))]
    args = [proj_b, proj_b, proj_b, gates_t, gates]
    if rev:
        in_specs += [pl.BlockSpec((chunk, d), lambda s: (cidx(s), 0)),
                     pl.BlockSpec((chunk, d), lambda s: (cidx(s), 0)),
                     pl.BlockSpec((1, d), lambda s: (0, 0))]
        args += [h_fwd, proj_f, norm_w]
    dh = d // heads
    return pl.pallas_call(
        functools.partial(_mlstm_body, lay=lay, rev=rev, heads=heads, chunk=chunk),
        grid=(nc,),
        in_specs=in_specs,
        out_specs=pl.BlockSpec((chunk, d), lambda s: (cidx(s), 0)),
        out_shape=jax.ShapeDtypeStruct((nt, d), BF16 if rev else F32),
        scratch_shapes=[pltpu.VMEM((heads, dh, dh), F32), pltpu.VMEM((heads, 1, dh), F32),
                        pltpu.VMEM((heads, 1, LANES), F32)],
        compiler_params=_cparams("arbitrary"),
        name="mlstm_bwd" if rev else "mlstm_fwd",
    )(*args)


def _na_bias_table(rpb):
    h = rpb.shape[0]
    cols = jnp.arange(GRID_W)
    cs = jnp.clip(cols - NA_KW // 2, 0, GRID_W - NA_KW)
    col_in = (cols[None, :] >= cs[:, None]) & (cols[None, :] < cs[:, None] + NA_KW)
    dc_idx = jnp.clip(cols[None, :] - cols[:, None], -(NA_KW - 1), NA_KW - 1) + NA_KW - 1
    t = jnp.where(col_in[None, None], rpb[:, :, dc_idx], NEG_INF)
    d_idx = jnp.arange(NA_KH)[:, None] + jnp.arange(NA_KH)[None, :]
    tc = t[:, d_idx]
    tc = tc.transpose(0, 1, 3, 2, 4).reshape(h, NA_KH, GRID_W, NA_KH * GRID_W)
    return tc.reshape(h // 2, 2, NA_KH, GRID_W, NA_KH * GRID_W).transpose(0, 2, 1, 3, 4).reshape(
        h // 2, NA_KH, 2 * GRID_W, NA_KH * GRID_W).astype(F32)


def _na_body(q_ref, k_ref, v_ref, bias_ref, o_ref, *, rows, rblk, dh):
    rb = pl.program_id(2)
    lane_q = lax.broadcasted_iota(jnp.int32, (GRID_W, 2 * dh), 1)
    qscale = dh ** -0.5

    def row(r, carry):
        rg = rb * rblk + r
        rs = jnp.clip(rg - NA_KH // 2, 0, rows - NA_KH)
        d0 = rs - rg + NA_KH - 1
        q = q_ref[pl.ds(pl.multiple_of(r * GRID_W, GRID_W), GRID_W), :].astype(F32) * qscale
        q2 = jnp.concatenate([jnp.where(lane_q < dh, q, 0.0), jnp.where(lane_q >= dh, q, 0.0)], axis=0).astype(BF16)
        k0 = pl.multiple_of(rs * GRID_W, GRID_W)
        kw = k_ref[pl.ds(k0, NA_KH * GRID_W), :]
        vw = v_ref[pl.ds(k0, NA_KH * GRID_W), :]
        s = _dot_nt(q2, kw) + bias_ref[0, d0]
        p = jnp.exp(s - jnp.max(s, axis=1, keepdims=True))
        o = _dot(p.astype(BF16), vw) / jnp.sum(p, axis=1, keepdims=True)
        out = jnp.where(lane_q < dh, o[:GRID_W], o[GRID_W:])
        o_ref[pl.ds(pl.multiple_of(r * GRID_W, GRID_W), GRID_W), :] = out.astype(o_ref.dtype)
        return carry

    lax.fori_loop(0, rblk, row, 0)


def _neigh_attn(proj_b, bias, d, tok_off, nseq, t, col0):
    rows = t // GRID_W
    rblk = 8
    dh = d // NA_HEADS
    pairs = NA_HEADS // 2
    pw = 2 * dh
    per = d // pw
    seq0 = tok_off // t
    qblk = rblk * GRID_W
    q0 = tok_off // qblk
    nrb = rows // rblk
    return pl.pallas_call(
        functools.partial(_na_body, rows=rows, rblk=rblk, dh=dh),
        grid=(pairs, nseq, nrb),
        in_specs=[pl.BlockSpec((qblk, pw), lambda p, b, r: (q0 + b * nrb + r, col0 * per + p)),
                  pl.BlockSpec((t, pw), lambda p, b, r: (seq0 + b, (col0 + 1) * per + p)),
                  pl.BlockSpec((t, pw), lambda p, b, r: (seq0 + b, (col0 + 2) * per + p)),
                  pl.BlockSpec((1, NA_KH, 2 * GRID_W, NA_KH * GRID_W), lambda p, b, r: (p, 0, 0, 0))],
        out_specs=pl.BlockSpec((qblk, pw), lambda p, b, r: (b * nrb + r, p)),
        out_shape=jax.ShapeDtypeStruct((nseq * t, d), BF16),
        compiler_params=_cparams("arbitrary", "arbitrary", "arbitrary"),
        name="neigh_attn",
    )(proj_b, proj_b, proj_b, bias)


def _softplus(x):
    return jnp.maximum(x, 0.0) + jnp.log(1.0 + jnp.exp(-jnp.abs(x)))


def _gelu_tanh(x):
    return 0.5 * x * (1.0 + jnp.tanh(0.7978845608028654 * (x + 0.044715 * (x * x * x))))


def _lru_body(*refs, lay, rev, tt):
    if rev:
        (x_ref, xp_ref, xn_ref, cw_ref, cb_ref, w_ref, ba_ref, bx_ref, lam_ref, hf_ref, y_ref,
         out_ref, carry_ref) = refs
    else:
        x_ref, xp_ref, xn_ref, cw_ref, cb_ref, w_ref, ba_ref, bx_ref, lam_ref, out_ref, carry_ref = refs
    s = pl.program_id(0)
    ti = (pl.num_programs(0) - 1 - s) if rev else s
    tok0 = ti * tt
    pos = lay.pos_in_seq(tok0)
    slen = lay.seq_len(tok0)
    is_first = pos == 0
    is_last = pos + tt == slen

    @pl.when(is_last if rev else is_first)
    def _():
        carry_ref[...] = jnp.zeros_like(carry_ref)

    d = x_ref.shape[1]
    x = x_ref[...]
    halo = SUBLANES
    xprev = jnp.where(is_first, 0.0, xp_ref[...])
    xnext = jnp.where(is_last, 0.0, xn_ref[...])
    xe = jnp.concatenate([xprev, x, xnext], axis=0)
    pad = CONV_W // 2
    xc = cb_ref[...]
    for j in range(CONV_W):
        o = halo - pad + j
        xc = xc + xe[o:o + tt] * cw_ref[j:j + 1, :]
    sp = _softplus(-lam_ref[...])
    ng = tt // SUBLANES
    sub = lax.broadcasted_iota(jnp.int32, (ng, SUBLANES, MXU_DIM), 1)
    for t in range(d // MXU_DIM):
        cs = slice(t * MXU_DIM, (t + 1) * MXU_DIM)
        xct = xc[:, cs]
        gg = _dot(xct.astype(BF16), w_ref[t])
        r = _sigmoid(gg[:, :MXU_DIM] + ba_ref[:, cs])
        i = _sigmoid(gg[:, MXU_DIM:] + bx_ref[:, cs])
        log_a = -LRU_C * r * sp[:, cs]
        a = jnp.exp(log_a)
        bt = jnp.sqrt(1.0 - a * a) * (i * xct)
        a3 = a.reshape(ng, SUBLANES, MXU_DIM)
        b3 = bt.reshape(ng, SUBLANES, MXU_DIM)
        k = 1
        while k < SUBLANES:
            if rev:
                ok = sub < SUBLANES - k
                ash = pltpu.roll(a3, SUBLANES - k, 1)
                bsh = pltpu.roll(b3, SUBLANES - k, 1)
            else:
                ok = sub >= k
                ash = pltpu.roll(a3, k, 1)
                bsh = pltpu.roll(b3, k, 1)
            b3 = jnp.where(ok, a3 * bsh + b3, b3)
            a3 = jnp.where(ok, a3 * ash, a3)
            k *= 2
        hprev = carry_ref[:, cs]
        outs = [None] * ng
        order = range(ng - 1, -1, -1) if rev else range(ng)
        for gidx in order:
            hg = a3[gidx] * hprev + b3[gidx]
            outs[gidx] = hg
            hprev = hg[0:1] if rev else hg[SUBLANES - 1:SUBLANES]
        carry_ref[:, cs] = hprev
        hh = jnp.concatenate(outs, axis=0)
        if rev:
            out_ref[:, cs] = ((hf_ref[:, cs] + hh) * _gelu_tanh(y_ref[:, cs])).astype(out_ref.dtype)
        else:
            out_ref[:, cs] = hh


def _lru_pass(lay, proj_f, conv_w, conv_b, w_dir, ba, bx, lam, h_fwd, rev, tt):
    nt, d = lay.nt, lay.d
    ntile = nt // tt
    hb = tt // SUBLANES
    nhb = nt // SUBLANES
    tidx = (lambda s: ntile - 1 - s) if rev else (lambda s: s)
    xcol = 1
    in_specs = [pl.BlockSpec((tt, d), lambda s: (tidx(s), xcol)),
                pl.BlockSpec((SUBLANES, d), lambda s: (jnp.maximum(tidx(s) * hb - 1, 0), xcol)),
                pl.BlockSpec((SUBLANES, d), lambda s: (jnp.minimum((tidx(s) + 1) * hb, nhb - 1), xcol)),
                pl.BlockSpec((CONV_W, d), lambda s: (0, 0)),
                pl.BlockSpec((1, d), lambda s: (0, 0)),
                pl.BlockSpec((d // MXU_DIM, MXU_DIM, 2 * MXU_DIM), lambda s: (0, 0, 0)),
                pl.BlockSpec((1, d), lambda s: (0, 0)),
                pl.BlockSpec((1, d), lambda s: (0, 0)),
                pl.BlockSpec((1, d), lambda s: (0, 0))]
    args = [proj_f, proj_f, proj_f, conv_w, conv_b, w_dir, ba, bx, lam]
    if rev:
        in_specs += [pl.BlockSpec((tt, d), lambda s: (tidx(s), 0)),
                     pl.BlockSpec((tt, d), lambda s: (tidx(s), 2))]
        args += [h_fwd, proj_f]
    return pl.pallas_call(
        functools.partial(_lru_body, lay=lay, rev=rev, tt=tt),
        grid=(ntile,),
        in_specs=in_specs,
        out_specs=pl.BlockSpec((tt, d), lambda s: (tidx(s), 0)),
        out_shape=jax.ShapeDtypeStruct((nt, d), BF16 if rev else F32),
        scratch_shapes=[pltpu.VMEM((1, d), F32)],
        compiler_params=_cparams("arbitrary"),
        name="rglru_bwd" if rev else "rglru_fwd",
    )(*args)


def _lru_gate_weights(wa, wx):
    nb, bw, _ = wa.shape
    per = MXU_DIM // bw
    nt = nb // per

    def diag(w):
        w = w.reshape(nt, per, bw, bw)
        eye = jnp.eye(per, dtype=w.dtype)
        return jnp.einsum('tpkj,pq->tpkqj', w, eye).reshape(nt, MXU_DIM, MXU_DIM)

    return jnp.concatenate([diag(wa), diag(wx)], axis=-1).astype(BF16)


def _merge_body(ha_ref, hb_ref, hc_ref, g0_ref, g1_ref, g2_ref, x_ref, mod_ref, wa_ref, wb_ref, wc_ref, wo_ref,
                nw_ref, wr_ref, br_ref, xo_ref, xm_ref, lg_ref):
    m = mod_ref[0]
    merged = (_sigmoid(g0_ref[...]) * _dot(ha_ref[...], wa_ref[...])
              + _sigmoid(g1_ref[...]) * _dot(hb_ref[...], wb_ref[...])
              + _sigmoid(g2_ref[...]) * _dot(hc_ref[...], wc_ref[...]))
    xn = x_ref[...] + m[2:3] * _dot(merged.astype(BF16), wo_ref[...])
    xo_ref[...] = xn
    xm = (_rms(xn, nw_ref[...]) * (1.0 + m[4:5]) + m[3:4]).astype(BF16)
    xm_ref[...] = xm
    lg_ref[...] = _dot_nt(wr_ref[...], xm) + br_ref[...]


def _merge_project(lay, h_a, h_b, h_c, proj_f, x, mod, wa, wb, wc, wo, norm_w, wr_t, br_t, tm):
    nt, d = x.shape
    ne = wr_t.shape[0]
    seq = lambda i: lay.seq_index(i * tm)
    act = lambda c: pl.BlockSpec((tm, d), lambda i: (i, c))
    const = lambda shape: pl.BlockSpec(shape, lambda i: (0,) * len(shape), pipeline_mode=pl.Buffered(1))
    return pl.pallas_call(
        _merge_body,
        grid=(nt // tm,),
        in_specs=[act(0), act(0), act(0), act(3), act(4), act(5), act(0),
                  pl.BlockSpec((1, N_MOD, d), lambda i: (seq(i), 0, 0)),
                  const((d, d)), const((d, d)), const((d, d)), const((d, d)),
                  const((1, d)), const((ne, d)), const((ne, 1))],
        out_specs=[pl.BlockSpec((tm, d), lambda i: (i, 0)),
                   pl.BlockSpec((tm, d), lambda i: (i, 0)),
                   pl.BlockSpec((ne, tm), lambda i: (0, i))],
        out_shape=[jax.ShapeDtypeStruct((nt, d), F32),
                   jax.ShapeDtypeStruct((nt, d), BF16),
                   jax.ShapeDtypeStruct((ne, nt), F32)],
        compiler_params=_cparams("arbitrary"),
        name="merge_outproj",
    )(h_a, h_b, h_c, proj_f, proj_f, proj_f, x, mod, wa, wb, wc, wo, norm_w, wr_t, br_t)


def _prefix_count(flags, tri_u, tri_l):
    ne, nr, nl = flags.shape
    fb = flags.astype(BF16)
    inc = _dot(fb.reshape(ne * nr, nl), tri_u).reshape(ne, nr, nl)
    tot = jnp.broadcast_to(inc[:, :, nl - 1:nl], (ne, nr, nl)).astype(BF16)
    offs = jnp.stack([_dot(tri_l, tot[e]) for e in range(ne)], axis=0)
    return offs + inc - flags


def _route_body(lg_ref, slot_ref, rank_ref, aff_ref, *, cap):
    lg = lg_ref[0]
    ne, nr, nl = lg.shape
    ex = jnp.exp(lg - jnp.max(lg, axis=0, keepdims=True))
    aff = ex / jnp.sum(ex, axis=0, keepdims=True)
    bits = pltpu.bitcast(aff, jnp.int32)

    def count(flags):
        return jnp.sum(jnp.sum(flags, axis=1, keepdims=True), axis=2, keepdims=True)

    def bisect(i, thr):
        cand = thr | lax.shift_left(jnp.int32(1), 30 - i)
        cnt = count(jnp.where(bits >= cand, 1.0, 0.0))
        return jnp.where(cnt >= cap, cand, thr)

    thr = lax.fori_loop(0, 31, bisect, jnp.zeros((ne, 1, 1), jnp.int32))
    li = lax.broadcasted_iota(jnp.int32, (nl, nl), 0)
    lj = lax.broadcasted_iota(jnp.int32, (nl, nl), 1)
    tri_u = jnp.where(li <= lj, 1.0, 0.0).astype(BF16)
    ri = lax.broadcasted_iota(jnp.int32, (nr, nr), 0)
    rj = lax.broadcasted_iota(jnp.int32, (nr, nr), 1)
    tri_l = jnp.where(rj < ri, 1.0, 0.0).astype(BF16)
    gt = jnp.where(bits > thr, 1.0, 0.0)
    eq = jnp.where(bits == thr, 1.0, 0.0)
    need = cap - count(gt)
    sel = gt + eq * jnp.where(_prefix_count(eq, tri_u, tri_l) < need, 1.0, 0.0)
    rank = _prefix_count(sel, tri_u, tri_l).astype(jnp.int32)
    rank_ref[0] = rank
    slot_ref[0] = jnp.where(sel > 0.5, rank, -1)
    aff_ref[0] = aff


def _route(logits_t, ng, cap):
    ne, nall = logits_t.shape
    n = nall // ng
    nr = n // LANES
    lg = logits_t.reshape(ne, ng, nr, LANES).transpose(1, 0, 2, 3)
    spec = pl.BlockSpec((1, ne, nr, LANES), lambda g: (g, 0, 0, 0))
    return pl.pallas_call(
        functools.partial(_route_body, cap=cap),
        grid=(ng,),
        in_specs=[spec],
        out_specs=[spec, spec, spec],
        out_shape=[jax.ShapeDtypeStruct((ng, ne, nr, LANES), jnp.int32),
                   jax.ShapeDtypeStruct((ng, ne, nr, LANES), jnp.int32),
                   jax.ShapeDtypeStruct((ng, ne, nr, LANES), F32)],
        compiler_params=_cparams("arbitrary"),
        name="ec_route",
    )(lg)


def _window_count(s0, s1):
    return jnp.where(s1 > s0, (s1 - 1) // LANES - s0 // LANES + 1, 0)


def _expert_body(st_ref, x_ref, slot_ref, wg_ref, wu_ref, wd_ref, ye_ref, xe_ref, *, nb, rt, fc):
    g, e, s = pl.program_id(0), pl.program_id(1), pl.program_id(2)
    ne = pl.num_programs(1)

    @pl.when(s == 0)
    def _():
        xe_ref[...] = jnp.zeros_like(xe_ref)

    @pl.when(s < nb)
    def _():
        base = (g * ne + e) * (nb + 1) + s
        s0, s1 = st_ref[base], st_ref[base + 1]
        w_lo = s0 // LANES
        slot = slot_ref[0, 0]
        xb = x_ref[0]
        sub = lax.broadcasted_iota(jnp.int32, (LANES, LANES), 0)

        def window(w, carry):
            row0 = pl.multiple_of((w_lo + w) * LANES, LANES)
            tgt = sub + row0
            onehot = jnp.concatenate(
                [jnp.where(slot[c:c + 1, :] == tgt, 1.0, 0.0).astype(BF16) for c in range(slot.shape[0])], axis=1)
            xe_ref[pl.ds(row0, LANES), :] += _dot(onehot, xb).astype(BF16)
            return carry

        lax.fori_loop(0, _window_count(s0, s1), window, 0)

    @pl.when(s >= nb)
    def _():
        r = s - nb
        xe = xe_ref[pl.ds(pl.multiple_of(r * rt, rt), rt), :]
        f = wg_ref.shape[2]
        acc = None
        for c in range(f // fc):
            fs = slice(c * fc, (c + 1) * fc)
            gg = _dot(xe, wg_ref[0, :, fs])
            hid = (gg * _sigmoid(gg) * _dot(xe, wu_ref[0, :, fs])).astype(BF16)
            part = _dot(hid, wd_ref[0, fs, :])
            acc = part if acc is None else acc + part
        ye_ref[0, 0] = acc.astype(ye_ref.dtype)


def _expert_ffn(starts, xm, slot, wg, wu, wd, cap, tb, rt):
    ng, n, d = xm.shape
    ne, _, f = wg.shape
    nb = n // tb
    nr = cap // rt
    grid_spec = pltpu.PrefetchScalarGridSpec(
        num_scalar_prefetch=1,
        grid=(ng, ne, nb + nr),
        in_specs=[pl.BlockSpec((1, tb, d), lambda g, e, s, st: (g, jnp.minimum(s, nb - 1), 0)),
                  pl.BlockSpec((1, 1, tb // LANES, LANES), lambda g, e, s, st: (g, e, jnp.minimum(s, nb - 1), 0)),
                  pl.BlockSpec((1, d, f), lambda g, e, s, st: (e, 0, 0)),
                  pl.BlockSpec((1, d, f), lambda g, e, s, st: (e, 0, 0)),
                  pl.BlockSpec((1, f, d), lambda g, e, s, st: (e, 0, 0))],
        out_specs=pl.BlockSpec((1, 1, rt, d), lambda g, e, s, st: (g, e, jnp.maximum(s - nb, 0), 0)),
        scratch_shapes=[pltpu.VMEM((cap, d), BF16)])
    return pl.pallas_call(
        functools.partial(_expert_body, nb=nb, rt=rt, fc=min(512, f)),
        grid_spec=grid_spec,
        out_shape=jax.ShapeDtypeStruct((ng, ne, cap, d), BF16),
        compiler_params=_cparams("arbitrary", "arbitrary", "arbitrary"),
        name="ec_expert_ffn",
    )(starts, xm, slot, wg, wu, wd)


def _combine_body(st_ref, x_ref, slot_ref, aff_ref, mod_ref, fw_ref, ye_hbm, o_ref, acc_ref, buf_ref, sem_ref,
                  *, lay, nb, tb, ne, wmax, final):
    g, b = pl.program_id(0), pl.program_id(1)

    def bounds(e):
        base = (g * ne + e) * (nb + 1) + b
        s0, s1 = st_ref[base], st_ref[base + 1]
        return s0 // LANES, _window_count(s0, s1)

    def copies(e, par):
        w_lo, nw = bounds(e)
        out = []
        for w in range(wmax):
            row0 = pl.multiple_of((w_lo + w) * LANES, LANES)
            cp = pltpu.make_async_copy(ye_hbm.at[g, e, pl.ds(row0, LANES), :], buf_ref.at[par, w], sem_ref.at[par, w])
            out.append((w < nw, cp))
        return out

    def start(e, par):
        for pred, cp in copies(e, par):
            @pl.when(pred)
            def _():
                cp.start()

    def wait(e, par):
        for pred, cp in copies(e, par):
            @pl.when(pred)
            def _():
                cp.wait()

    acc_ref[...] = jnp.zeros_like(acc_ref)
    slot_all = slot_ref[0]
    aff_all = aff_ref[0]
    lane = lax.broadcasted_iota(jnp.int32, (tb, LANES), 1)
    start(0, 0)
    for e in range(ne):
        par = e % 2
        if e + 1 < ne:
            start(e + 1, 1 - par)
        wait(e, par)
        w_lo, nw = bounds(e)
        slot_col = slot_all[:, e:e + 1]
        gate_col = aff_all[:, e:e + 1]

        def window(w, carry):
            tgt = lane + (w_lo + w) * LANES
            onehot = jnp.where(slot_col == tgt, 1.0, 0.0).astype(BF16)
            acc_ref[...] += gate_col * _dot(onehot, buf_ref[par, w])
            return carry

        lax.fori_loop(0, nw, window, 0)

    m = mod_ref[0]
    xn = x_ref[...] + m[5:6] * acc_ref[...]
    o_ref[...] = _rms(xn, fw_ref[...]) if final else xn


def _combine(lay, starts, x, slot_t, aff_t, mod, final_w, ye, tb, final):
    nt, d = x.shape
    ng, n, ne = slot_t.shape
    nb = n // tb
    wmax = tb // LANES + 1
    grid_spec = pltpu.PrefetchScalarGridSpec(
        num_scalar_prefetch=1,
        grid=(ng, nb),
        in_specs=[pl.BlockSpec((tb, d), lambda g, b, st: (g * nb + b, 0)),
                  pl.BlockSpec((1, tb, ne), lambda g, b, st: (g, b, 0)),
                  pl.BlockSpec((1, tb, ne), lambda g, b, st: (g, b, 0)),
                  pl.BlockSpec((1, N_MOD, d), lambda g, b, st: (lay.seq_index((g * nb + b) * tb), 0, 0)),
                  pl.BlockSpec((1, d), lambda g, b, st: (0, 0)),
                  pl.BlockSpec(memory_space=pl.ANY)],
        out_specs=pl.BlockSpec((tb, d), lambda g, b, st: (g * nb + b, 0)),
        scratch_shapes=[pltpu.VMEM((tb, d), F32), pltpu.VMEM((2, wmax, LANES, d), BF16),
                        pltpu.SemaphoreType.DMA((2, wmax))])
    return pl.pallas_call(
        functools.partial(_combine_body, lay=lay, nb=nb, tb=tb, ne=ne, wmax=wmax, final=final),
        grid_spec=grid_spec,
        out_shape=jax.ShapeDtypeStruct((nt, d), F32),
        compiler_params=_cparams("arbitrary", "arbitrary"),
        name="ec_combine",
    )(starts, x, slot_t, aff_t, mod, final_w, ye)


def _block_starts(rank, tb, cap):
    ng, ne = rank.shape[:2]
    first = rank[:, :, ::tb // LANES, 0]
    return jnp.concatenate([first, jnp.full((ng, ne, 1), cap, jnp.int32)], axis=2).reshape(-1)


def kernel(x_prompt, x_sample, c_prompt, c_sample, norm1_w, norm2_w, w_mod, b_mod, w_in, b_in, mlstm_norm_w, na_rpb, conv_w, conv_b, lru_wa, lru_ba, lru_wx, lru_bx, lru_L, w_br_a, w_br_b, w_br_c, w_out, w_router, b_router, w_gate_e, w_up_e, w_down_e, final_norm_w):
    bp, tp, d = x_prompt.shape
    bs, ts, _ = x_sample.shape
    lay = Layout(bp, tp, bs, ts, d)
    assert lay.n_p == lay.n_s, "the two request groups are routed as equal-sized token sets"
    depth = w_in.shape[0]
    ne = w_router.shape[2]
    ng, n = 2, lay.n_p
    cap = EC_FACTOR * n // ne
    tm_in = min(1024, ts)
    tm_merge = min(512, ts)
    tt_lru = min(256, ts)
    tb_gather = min(1024, n)
    tb_comb = min(512, ts)
    rt = min(512, cap)

    x = jnp.concatenate([x_prompt.reshape(lay.n_p, d), x_sample.reshape(lay.n_s, d)], axis=0)
    nseq = bp + bs
    rows_c = -(-nseq // SUBLANES) * SUBLANES
    c_all = jnp.zeros((rows_c, d), F32).at[:nseq].set(jnp.concatenate([c_prompt, c_sample], axis=0))
    mod_all = _modulation(c_all, w_mod, b_mod).reshape(depth, rows_c, N_MOD, d)

    widths = (d, d, d, d, 4 * MLSTM_HEADS, d, d, d, d, d, N_BRANCH * d)
    offs = [0]
    for w in widths:
        offs.append(offs[-1] + w)
    col = lambda a, i: a[..., offs[i]:offs[i + 1]]
    bcols = (0, 1, 2, 5, 6, 7)
    fcols = (3, 8, 9, 10)

    for l in range(depth):
        wl, bl = w_in[l], b_in[l]
        wb = jnp.concatenate([col(wl, i) for i in bcols], axis=1).astype(BF16)
        bb = jnp.concatenate([col(bl, i) for i in bcols], axis=0).reshape(1, -1)
        wf = jnp.concatenate([col(wl, i) for i in fcols], axis=1).astype(BF16)
        bf = jnp.concatenate([col(bl, i) for i in fcols], axis=0).reshape(1, -1)
        ngate = 4 * MLSTM_HEADS
        wg = jnp.zeros((d, LANES), BF16).at[:, :ngate].set(col(wl, 4).astype(BF16))
        bg = jnp.zeros((1, LANES), F32).at[0, :ngate].set(col(bl, 4))
        mod = mod_all[l]

        proj_b, proj_f, gates = _input_projection(lay, x, mod, norm1_w[l].reshape(1, d), wb, bb, wf, bf, wg, bg, tm_in)
        gates_t = gates[:, :ngate].T

        nw = mlstm_norm_w[l].reshape(1, d)
        h_fwd = _mlstm_pass(lay, proj_b, proj_f, gates, gates_t, nw, None, rev=False)
        h_a = _mlstm_pass(lay, proj_b, proj_f, gates, gates_t, nw, h_fwd, rev=True)

        bias = _na_bias_table(na_rpb[l])
        h_b = jnp.concatenate([_neigh_attn(proj_b, bias, d, 0, bp, tp, 3),
                               _neigh_attn(proj_b, bias, d, lay.n_p, bs, ts, 3)], axis=0)

        lru_args = (conv_w[l], conv_b[l].reshape(1, d))
        l_fwd = _lru_pass(lay, proj_f, *lru_args, _lru_gate_weights(lru_wa[l, 0], lru_wx[l, 0]),
                          lru_ba[l, 0].reshape(1, d), lru_bx[l, 0].reshape(1, d), lru_L[l, 0].reshape(1, d),
                          None, rev=False, tt=tt_lru)
        h_c = _lru_pass(lay, proj_f, *lru_args, _lru_gate_weights(lru_wa[l, 1], lru_wx[l, 1]),
                        lru_ba[l, 1].reshape(1, d), lru_bx[l, 1].reshape(1, d), lru_L[l, 1].reshape(1, d),
                        l_fwd, rev=True, tt=tt_lru)

        x, xm2, logits_t = _merge_project(
            lay, h_a, h_b, h_c, proj_f, x, mod, w_br_a[l].astype(BF16), w_br_b[l].astype(BF16),
            w_br_c[l].astype(BF16), w_out[l].astype(BF16), norm2_w[l].reshape(1, d),
            w_router[l].T.astype(BF16), b_router[l].reshape(ne, 1), tm_merge)

        slot, rank, aff = _route(logits_t, ng, cap)
        ye = _expert_ffn(_block_starts(rank, tb_gather, cap), xm2.reshape(ng, n, d), slot,
                         w_gate_e[l].astype(BF16), w_up_e[l].astype(BF16), w_down_e[l].astype(BF16),
                         cap, tb_gather, rt)
        slot_t = slot.reshape(ng, ne, n).transpose(0, 2, 1)
        aff_t = aff.reshape(ng, ne, n).transpose(0, 2, 1)
        x = _combine(lay, _block_starts(rank, tb_comb, cap), x, slot_t, aff_t, mod, final_norm_w.reshape(1, d), ye,
                     tb_comb, final=(l == depth - 1))

    return (x[:lay.n_p].reshape(bp, tp, d), x[lay.n_p:].reshape(bs, ts, d))
```

```python
import functools
from typing import NamedTuple

import jax
import jax.numpy as jnp
from jax import lax
from jax.experimental import pallas as pl
from jax.experimental.pallas import tpu as pltpu

EPS = 1e-6
NEG_INF = -1e30
GRID_W = 64
MLSTM_HEADS = 4
MLSTM_CHUNK = 128
NA_HEADS = 16
NA_KH = 8
NA_KW = 16
LRU_BLOCKS = 16
LRU_C = 8.0
CONV_W = 4
N_BRANCH = 3
N_EXPERTS = 16
EC_FACTOR = 2
N_MOD = 6

LANES = 128
SUBLANES = 8
MXU_DIM = 256
VMEM_LIMIT = 56 * 1024 * 1024

F32 = jnp.float32
BF16 = jnp.bfloat16


class Layout(NamedTuple):
    bp: int
    tp: int
    bs: int
    ts: int
    d: int

    @property
    def n_p(self):
        return self.bp * self.tp

    @property
    def n_s(self):
        return self.bs * self.ts

    @property
    def nt(self):
        return self.n_p + self.n_s

    def seq_index(self, tok0):
        return jnp.where(tok0 < self.n_p, tok0 // self.tp, self.bp + (tok0 - self.n_p) // self.ts)

    def seq_len(self, tok0):
        return jnp.where(tok0 < self.n_p, self.tp, self.ts)

    def pos_in_seq(self, tok0):
        return jnp.where(tok0 < self.n_p, tok0 % self.tp, (tok0 - self.n_p) % self.ts)


def _cparams(*sem):
    return pltpu.CompilerParams(dimension_semantics=sem, vmem_limit_bytes=VMEM_LIMIT)


def _dot(a, b):
    return jnp.dot(a, b, preferred_element_type=F32)


def _dot_nt(a, b):
    return lax.dot_general(a, b, (((1,), (1,)), ((), ())), preferred_element_type=F32)


def _dot_tn(a, b):
    return lax.dot_general(a, b, (((0,), (0,)), ((), ())), preferred_element_type=F32)


def _sigmoid(x):
    return 1.0 / (1.0 + jnp.exp(-x))


def _rms(x, w):
    return x * lax.rsqrt(jnp.mean(x * x, axis=-1, keepdims=True) + EPS) * w


def _mod_body(c_ref, w_ref, b_ref, o_ref):
    c = c_ref[...]
    ca = (c * _sigmoid(c)).astype(BF16)
    o_ref[0] = _dot(ca, w_ref[0].astype(BF16)) + b_ref[0]


def _modulation(c_all, w_mod, b_mod):
    depth, d, nm = w_mod.shape
    rows = c_all.shape[0]
    tn = 1536
    return pl.pallas_call(
        _mod_body,
        grid=(depth, nm // tn),
        in_specs=[pl.BlockSpec((rows, d), lambda l, j: (0, 0)),
                  pl.BlockSpec((1, d, tn), lambda l, j: (l, 0, j)),
                  pl.BlockSpec((1, 1, tn), lambda l, j: (l, 0, j))],
        out_specs=pl.BlockSpec((1, rows, tn), lambda l, j: (l, 0, j)),
        out_shape=jax.ShapeDtypeStruct((depth, rows, nm), F32),
        compiler_params=_cparams("arbitrary", "arbitrary"),
        name="adaln_modulation",
    )(c_all, w_mod, b_mod.reshape(depth, 1, nm))


def _inproj_body(x_ref, mod_ref, nw_ref, wb_ref, bb_ref, wf_ref, bf_ref, wg_ref, bg_ref,
                 ob_ref, of_ref, og_ref, xm_ref, *, n_bf):
    j = pl.program_id(1)

    @pl.when(j == 0)
    def _():
        m = mod_ref[0]
        xm = (_rms(x_ref[...], nw_ref[...]) * (1.0 + m[1:2]) + m[0:1]).astype(BF16)
        xm_ref[...] = xm
        og_ref[...] = _dot(xm, wg_ref[...]) + bg_ref[...]

    @pl.when(j < n_bf)
    def _():
        ob_ref[...] = (_dot(xm_ref[...], wb_ref[...]) + bb_ref[...]).astype(BF16)

    @pl.when(j >= n_bf)
    def _():
        of_ref[...] = _dot(xm_ref[...], wf_ref[...]) + bf_ref[...]


def _input_projection(lay, x, mod, norm_w, wb, bb, wf, bf, wg, bg, tm):
    nt, d = x.shape
    n_bf, n_f = wb.shape[1] // d, wf.shape[1] // d
    seq = lambda i: lay.seq_index(i * tm)
    return pl.pallas_call(
        functools.partial(_inproj_body, n_bf=n_bf),
        grid=(nt // tm, n_bf + n_f),
        in_specs=[pl.BlockSpec((tm, d), lambda i, j: (i, 0)),
                  pl.BlockSpec((1, N_MOD, d), lambda i, j: (seq(i), 0, 0)),
                  pl.BlockSpec((1, d), lambda i, j: (0, 0)),
                  pl.BlockSpec((d, d), lambda i, j: (0, jnp.minimum(j, n_bf - 1))),
                  pl.BlockSpec((1, d), lambda i, j: (0, jnp.minimum(j, n_bf - 1))),
                  pl.BlockSpec((d, d), lambda i, j: (0, jnp.maximum(j - n_bf, 0))),
                  pl.BlockSpec((1, d), lambda i, j: (0, jnp.maximum(j - n_bf, 0))),
                  pl.BlockSpec((d, LANES), lambda i, j: (0, 0)),
                  pl.BlockSpec((1, LANES), lambda i, j: (0, 0))],
        out_specs=[pl.BlockSpec((tm, d), lambda i, j: (i, jnp.minimum(j, n_bf - 1))),
                   pl.BlockSpec((tm, d), lambda i, j: (i, jnp.maximum(j - n_bf, 0))),
                   pl.BlockSpec((tm, LANES), lambda i, j: (i, 0))],
        out_shape=[jax.ShapeDtypeStruct((nt, n_bf * d), BF16),
                   jax.ShapeDtypeStruct((nt, n_f * d), F32),
                   jax.ShapeDtypeStruct((nt, LANES), F32)],
        scratch_shapes=[pltpu.VMEM((tm, d), BF16)],
        compiler_params=_cparams("arbitrary", "arbitrary"),
        name="norm_inproj",
    )(x, mod, norm_w, wb, bb, wf, bf, wg, bg)


def _lane_cumsum(x, rev):
    n = x.shape[-1]
    lane = lax.broadcasted_iota(jnp.int32, x.shape, 1)
    k = 1
    while k < n:
        if rev:
            x = x + jnp.where(lane < n - k, pltpu.roll(x, n - k, 1), 0.0)
        else:
            x = x + jnp.where(lane >= k, pltpu.roll(x, k, 1), 0.0)
        k *= 2
    return x


def _log_sigmoid(x):
    return jnp.minimum(x, 0.0) - jnp.log(1.0 + jnp.exp(-jnp.abs(x)))


def _mlstm_body(*refs, lay, rev, heads, chunk):
    if rev:
        q_ref, k_ref, v_ref, gr_ref, gc_ref, hf_ref, o_ref, nw_ref, out_ref, c_ref, n_ref, m_ref = refs
    else:
        q_ref, k_ref, v_ref, gr_ref, gc_ref, out_ref, c_ref, n_ref, m_ref = refs
    s = pl.program_id(0)
    ci = (pl.num_programs(0) - 1 - s) if rev else s
    tok0 = ci * chunk
    pos = lay.pos_in_seq(tok0)
    first = (pos + chunk == lay.seq_len(tok0)) if rev else (pos == 0)

    @pl.when(first)
    def _():
        c_ref[...] = jnp.zeros_like(c_ref)
        n_ref[...] = jnp.zeros_like(n_ref)
        m_ref[...] = jnp.zeros_like(m_ref)

    dh = q_ref.shape[1] // heads
    gr = gr_ref[...]
    gc = gc_ref[...]
    gi = 2 * heads if rev else 0
    lf_all = _log_sigmoid(gr)
    b_all = _lane_cumsum(lf_all, rev)
    ri = lax.broadcasted_iota(jnp.int32, (chunk, chunk), 0)
    cj = lax.broadcasted_iota(jnp.int32, (chunk, chunk), 1)
    mask = (cj >= ri) if rev else (cj <= ri)
    kscale = dh ** -0.5
    for h in range(heads):
        sl = slice(h * dh, (h + 1) * dh)
        q = q_ref[:, sl]
        k = (k_ref[:, sl].astype(F32) * kscale).astype(BF16)
        v = v_ref[:, sl]
        ig_row = gr[gi + h:gi + h + 1, :]
        lf_row = lf_all[gi + heads + h:gi + heads + h + 1, :]
        b_row = b_all[gi + heads + h:gi + heads + h + 1, :]
        ig_col = gc[:, gi + h:gi + h + 1]
        m_old = m_ref[h][:, 0:1]
        b_col = jnp.sum(jnp.where(mask, lf_row, 0.0), axis=1, keepdims=True)
        dmat = jnp.where(mask, b_col - b_row + ig_row, NEG_INF)
        g = b_col + m_old
        m_row = jnp.maximum(g, jnp.max(dmat, axis=1, keepdims=True))
        w_intra = jnp.exp(dmat - m_row)
        w_inter = jnp.exp(g - m_row)
        sc = _dot_nt(q, k) * w_intra
        c_old = c_ref[h]
        n_old = n_ref[h]
        qf = q.astype(F32)
        num = _dot(sc.astype(BF16), v) + w_inter * _dot(q, c_old.astype(BF16))
        den = jnp.sum(sc, axis=1, keepdims=True) + w_inter * jnp.sum(qf * n_old, axis=1, keepdims=True)
        hh = num / jnp.maximum(jnp.abs(den), jnp.exp(-m_row))
        b_last = b_row[:, 0:1] if rev else b_row[:, chunk - 1:chunk]
        w_last_row = b_last - b_row + ig_row
        m_new = jnp.maximum(b_last + m_old, jnp.max(w_last_row, axis=1, keepdims=True))
        w_k = jnp.exp(b_last - b_col + ig_col - m_new)
        decay = jnp.exp(b_last + m_old - m_new)
        kw = k.astype(F32) * w_k
        c_ref[h] = decay * c_old + _dot_tn(kw.astype(BF16), v)
        n_ref[h] = decay * n_old + jnp.sum(kw, axis=0, keepdims=True)
        m_ref[h] = jnp.broadcast_to(m_new, m_ref.shape[1:])
        if rev:
            hs = hf_ref[:, sl] + hh
            mu = jnp.mean(hs, axis=1, keepdims=True)
            var = jnp.mean(jnp.square(hs - mu), axis=1, keepdims=True)
            hn = (hs - mu) * lax.rsqrt(var + EPS) * nw_ref[:, sl]
            out_ref[:, sl] = (_sigmoid(o_ref[:, sl]) * hn).astype(out_ref.dtype)
        else:
            out_ref[:, sl] = hh


def _mlstm_pass(lay, proj_b, proj_f, gates, gates_t, norm_w, h_fwd, rev):
    nt, d = lay.nt, lay.d
    chunk, heads = MLSTM_CHUNK, MLSTM_HEADS
    nc = nt // chunk
    cidx = (lambda s: nc - 1 - s) if rev else (lambda s: s)
    in_specs = [pl.BlockSpec((chunk, d), lambda s: (cidx(s), 0)),
                pl.BlockSpec((chunk, d), lambda s: (cidx(s), 1)),
                pl.BlockSpec((chunk, d), lambda s: (cidx(s), 2)),
                pl.BlockSpec((4 * heads, chunk), lambda s: (0, cidx(s))),
                pl.BlockSpec((chunk, LANES), lambda s: (cidx(s), 0))]
    args = [proj_b, proj_b, proj_b, gates_t, gates]
    if rev:
        in_specs += [pl.BlockSpec((chunk, d), lambda s: (cidx(s), 0)),
                     pl.BlockSpec((chunk, d), lambda s: (cidx(s), 0)),
                     pl.BlockSpec((1, d), lambda s: (0, 0))]
        args += [h_fwd, proj_f, norm_w]
    dh = d // heads
    return pl.pallas_call(
        functools.partial(_mlstm_body, lay=lay, rev=rev, heads=heads, chunk=chunk),
        grid=(nc,),
        in_specs=in_specs,
        out_specs=pl.BlockSpec((chunk, d), lambda s: (cidx(s), 0)),
        out_shape=jax.ShapeDtypeStruct((nt, d), BF16 if rev else F32),
        scratch_shapes=[pltpu.VMEM((heads, dh, dh), F32), pltpu.VMEM((heads, 1, dh), F32),
                        pltpu.VMEM((heads, 1, LANES), F32)],
        compiler_params=_cparams("arbitrary"),
        name="mlstm_bwd" if rev else "mlstm_fwd",
    )(*args)


def _na_bias_table(rpb):
    h = rpb.shape[0]
    cols = jnp.arange(GRID_W)
    cs = jnp.clip(cols - NA_KW // 2, 0, GRID_W - NA_KW)
    col_in = (cols[None, :] >= cs[:, None]) & (cols[None, :] < cs[:, None] + NA_KW)
    dc_idx = jnp.clip(cols[None, :] - cols[:, None], -(NA_KW - 1), NA_KW - 1) + NA_KW - 1
    t = jnp.where(col_in[None, None], rpb[:, :, dc_idx], NEG_INF)
    d_idx = jnp.arange(NA_KH)[:, None] + jnp.arange(NA_KH)[None, :]
    tc = t[:, d_idx]
    tc = tc.transpose(0, 1, 3, 2, 4).reshape(h, NA_KH, GRID_W, NA_KH * GRID_W)
    return tc.reshape(h // 2, 2, NA_KH, GRID_W, NA_KH * GRID_W).transpose(0, 2, 1, 3, 4).reshape(
        h // 2, NA_KH, 2 * GRID_W, NA_KH * GRID_W).astype(F32)


def _na_body(q_ref, k_ref, v_ref, bias_ref, o_ref, s_ref, p_ref, l_ref, *, rows, rblk, dh):
    rb = pl.program_id(2)
    lane_q = lax.broadcasted_iota(jnp.int32, (GRID_W, 2 * dh), 1)
    qscale = dh ** -0.5

    starts, scores = [], []
    for r in range(rblk):
        rg = rb * rblk + r
        rs = jnp.clip(rg - NA_KH // 2, 0, rows - NA_KH)
        d0 = rs - rg + NA_KH - 1
        q = q_ref[r * GRID_W:(r + 1) * GRID_W, :].astype(F32) * qscale
        q2 = jnp.concatenate([jnp.where(lane_q < dh, q, 0.0), jnp.where(lane_q >= dh, q, 0.0)], axis=0).astype(BF16)
        k0 = pl.multiple_of(rs * GRID_W, GRID_W)
        starts.append(k0)
        s_ref[r] = _dot_nt(q2, k_ref[pl.ds(k0, NA_KH * GRID_W), :]) + bias_ref[0, d0]
    for r in range(rblk):
        s = s_ref[r]
        p = jnp.exp(s - jnp.max(s, axis=1, keepdims=True))
        l_ref[r] = jnp.broadcast_to(jnp.sum(p, axis=1, keepdims=True), l_ref.shape[1:])
        p_ref[r] = p.astype(BF16)
    for r in range(rblk):
        o = _dot(p_ref[r], v_ref[pl.ds(starts[r], NA_KH * GRID_W), :]) / l_ref[r]
        out = jnp.where(lane_q < dh, o[:GRID_W], o[GRID_W:])
        o_ref[r * GRID_W:(r + 1) * GRID_W, :] = out.astype(o_ref.dtype)


def _neigh_attn(proj_b, bias, d, tok_off, nseq, t, col0):
    rows = t // GRID_W
    rblk = 8
    dh = d // NA_HEADS
    pairs = NA_HEADS // 2
    pw = 2 * dh
    per = d // pw
    seq0 = tok_off // t
    qblk = rblk * GRID_W
    q0 = tok_off // qblk
    nrb = rows // rblk
    return pl.pallas_call(
        functools.partial(_na_body, rows=rows, rblk=rblk, dh=dh),
        grid=(pairs, nseq, nrb),
        in_specs=[pl.BlockSpec((qblk, pw), lambda p, b, r: (q0 + b * nrb + r, col0 * per + p)),
                  pl.BlockSpec((t, pw), lambda p, b, r: (seq0 + b, (col0 + 1) * per + p)),
                  pl.BlockSpec((t, pw), lambda p, b, r: (seq0 + b, (col0 + 2) * per + p)),
                  pl.BlockSpec((1, NA_KH, 2 * GRID_W, NA_KH * GRID_W), lambda p, b, r: (p, 0, 0, 0))],
        out_specs=pl.BlockSpec((qblk, pw), lambda p, b, r: (b * nrb + r, p)),
        out_shape=jax.ShapeDtypeStruct((nseq * t, d), BF16),
        scratch_shapes=[pltpu.VMEM((rblk, 2 * GRID_W, NA_KH * GRID_W), F32),
                        pltpu.VMEM((rblk, 2 * GRID_W, NA_KH * GRID_W), BF16),
                        pltpu.VMEM((rblk, 2 * GRID_W, LANES), F32)],
        compiler_params=_cparams("arbitrary", "arbitrary", "arbitrary"),
        name="neigh_attn",
    )(proj_b, proj_b, proj_b, bias)


def _softplus(x):
    return jnp.maximum(x, 0.0) + jnp.log(1.0 + jnp.exp(-jnp.abs(x)))


def _gelu_tanh(x):
    return 0.5 * x * (1.0 + jnp.tanh(0.7978845608028654 * (x + 0.044715 * (x * x * x))))


def _lru_body(*refs, lay, rev, tt):
    if rev:
        (x_ref, xp_ref, xn_ref, cw_ref, cb_ref, w_ref, ba_ref, bx_ref, lam_ref, hf_ref, y_ref,
         out_ref, carry_ref) = refs
    else:
        x_ref, xp_ref, xn_ref, cw_ref, cb_ref, w_ref, ba_ref, bx_ref, lam_ref, out_ref, carry_ref = refs
    s = pl.program_id(0)
    ti = (pl.num_programs(0) - 1 - s) if rev else s
    tok0 = ti * tt
    pos = lay.pos_in_seq(tok0)
    slen = lay.seq_len(tok0)
    is_first = pos == 0
    is_last = pos + tt == slen

    @pl.when(is_last if rev else is_first)
    def _():
        carry_ref[...] = jnp.zeros_like(carry_ref)

    d = x_ref.shape[1]
    x = x_ref[...]
    halo = SUBLANES
    xprev = jnp.where(is_first, 0.0, xp_ref[...])
    xnext = jnp.where(is_last, 0.0, xn_ref[...])
    xe = jnp.concatenate([xprev, x, xnext], axis=0)
    pad = CONV_W // 2
    xc = cb_ref[...]
    for j in range(CONV_W):
        o = halo - pad + j
        xc = xc + xe[o:o + tt] * cw_ref[j:j + 1, :]
    sp = _softplus(-lam_ref[...])
    ng = tt // SUBLANES
    sub = lax.broadcasted_iota(jnp.int32, (ng, SUBLANES, MXU_DIM), 1)
    for t in range(d // MXU_DIM):
        cs = slice(t * MXU_DIM, (t + 1) * MXU_DIM)
        xct = xc[:, cs]
        gg = _dot(xct.astype(BF16), w_ref[t])
        r = _sigmoid(gg[:, :MXU_DIM] + ba_ref[:, cs])
        i = _sigmoid(gg[:, MXU_DIM:] + bx_ref[:, cs])
        log_a = -LRU_C * r * sp[:, cs]
        a = jnp.exp(log_a)
        bt = jnp.sqrt(1.0 - a * a) * (i * xct)
        a3 = a.reshape(ng, SUBLANES, MXU_DIM)
        b3 = bt.reshape(ng, SUBLANES, MXU_DIM)
        k = 1
        while k < SUBLANES:
            if rev:
                ok = sub < SUBLANES - k
                ash = pltpu.roll(a3, SUBLANES - k, 1)
                bsh = pltpu.roll(b3, SUBLANES - k, 1)
            else:
                ok = sub >= k
                ash = pltpu.roll(a3, k, 1)
                bsh = pltpu.roll(b3, k, 1)
            b3 = jnp.where(ok, a3 * bsh + b3, b3)
            a3 = jnp.where(ok, a3 * ash, a3)
            k *= 2
        hprev = carry_ref[:, cs]
        outs = [None] * ng
        order = range(ng - 1, -1, -1) if rev else range(ng)
        for gidx in order:
            hg = a3[gidx] * hprev + b3[gidx]
            outs[gidx] = hg
            hprev = hg[0:1] if rev else hg[SUBLANES - 1:SUBLANES]
        carry_ref[:, cs] = hprev
        hh = jnp.concatenate(outs, axis=0)
        if rev:
            out_ref[:, cs] = ((hf_ref[:, cs] + hh) * _gelu_tanh(y_ref[:, cs])).astype(out_ref.dtype)
        else:
            out_ref[:, cs] = hh


def _lru_pass(lay, proj_f, conv_w, conv_b, w_dir, ba, bx, lam, h_fwd, rev, tt):
    nt, d = lay.nt, lay.d
    ntile = nt // tt
    hb = tt // SUBLANES
    nhb = nt // SUBLANES
    tidx = (lambda s: ntile - 1 - s) if rev else (lambda s: s)
    xcol = 1
    in_specs = [pl.BlockSpec((tt, d), lambda s: (tidx(s), xcol)),
                pl.BlockSpec((SUBLANES, d), lambda s: (jnp.maximum(tidx(s) * hb - 1, 0), xcol)),
                pl.BlockSpec((SUBLANES, d), lambda s: (jnp.minimum((tidx(s) + 1) * hb, nhb - 1), xcol)),
                pl.BlockSpec((CONV_W, d), lambda s: (0, 0)),
                pl.BlockSpec((1, d), lambda s: (0, 0)),
                pl.BlockSpec((d // MXU_DIM, MXU_DIM, 2 * MXU_DIM), lambda s: (0, 0, 0)),
                pl.BlockSpec((1, d), lambda s: (0, 0)),
                pl.BlockSpec((1, d), lambda s: (0, 0)),
                pl.BlockSpec((1, d), lambda s: (0, 0))]
    args = [proj_f, proj_f, proj_f, conv_w, conv_b, w_dir, ba, bx, lam]
    if rev:
        in_specs += [pl.BlockSpec((tt, d), lambda s: (tidx(s), 0)),
                     pl.BlockSpec((tt, d), lambda s: (tidx(s), 2))]
        args += [h_fwd, proj_f]
    return pl.pallas_call(
        functools.partial(_lru_body, lay=lay, rev=rev, tt=tt),
        grid=(ntile,),
        in_specs=in_specs,
        out_specs=pl.BlockSpec((tt, d), lambda s: (tidx(s), 0)),
        out_shape=jax.ShapeDtypeStruct((nt, d), BF16 if rev else F32),
        scratch_shapes=[pltpu.VMEM((1, d), F32)],
        compiler_params=_cparams("arbitrary"),
        name="rglru_bwd" if rev else "rglru_fwd",
    )(*args)


def _lru_gate_weights(wa, wx):
    nb, bw, _ = wa.shape
    per = MXU_DIM // bw
    nt = nb // per

    def diag(w):
        w = w.reshape(nt, per, bw, bw)
        eye = jnp.eye(per, dtype=w.dtype)
        return jnp.einsum('tpkj,pq->tpkqj', w, eye).reshape(nt, MXU_DIM, MXU_DIM)

    return jnp.concatenate([diag(wa), diag(wx)], axis=-1).astype(BF16)


def _merge_body(ha_ref, hb_ref, hc_ref, g0_ref, g1_ref, g2_ref, x_ref, mod_ref, wa_ref, wb_ref, wc_ref, wo_ref,
                nw_ref, wr_ref, br_ref, xo_ref, xm_ref, lg_ref):
    m = mod_ref[0]
    merged = (_sigmoid(g0_ref[...]) * _dot(ha_ref[...], wa_ref[...])
              + _sigmoid(g1_ref[...]) * _dot(hb_ref[...], wb_ref[...])
              + _sigmoid(g2_ref[...]) * _dot(hc_ref[...], wc_ref[...]))
    xn = x_ref[...] + m[2:3] * _dot(merged.astype(BF16), wo_ref[...])
    xo_ref[...] = xn
    xm = (_rms(xn, nw_ref[...]) * (1.0 + m[4:5]) + m[3:4]).astype(BF16)
    xm_ref[...] = xm
    lg_ref[...] = _dot_nt(wr_ref[...], xm) + br_ref[...]


def _merge_project(lay, h_a, h_b, h_c, proj_f, x, mod, wa, wb, wc, wo, norm_w, wr_t, br_t, tm):
    nt, d = x.shape
    ne = wr_t.shape[0]
    seq = lambda i: lay.seq_index(i * tm)
    act = lambda c: pl.BlockSpec((tm, d), lambda i: (i, c))
    const = lambda shape: pl.BlockSpec(shape, lambda i: (0,) * len(shape), pipeline_mode=pl.Buffered(1))
    return pl.pallas_call(
        _merge_body,
        grid=(nt // tm,),
        in_specs=[act(0), act(0), act(0), act(3), act(4), act(5), act(0),
                  pl.BlockSpec((1, N_MOD, d), lambda i: (seq(i), 0, 0)),
                  const((d, d)), const((d, d)), const((d, d)), const((d, d)),
                  const((1, d)), const((ne, d)), const((ne, 1))],
        out_specs=[pl.BlockSpec((tm, d), lambda i: (i, 0)),
                   pl.BlockSpec((tm, d), lambda i: (i, 0)),
                   pl.BlockSpec((ne, tm), lambda i: (0, i))],
        out_shape=[jax.ShapeDtypeStruct((nt, d), F32),
                   jax.ShapeDtypeStruct((nt, d), BF16),
                   jax.ShapeDtypeStruct((ne, nt), F32)],
        compiler_params=_cparams("arbitrary"),
        name="merge_outproj",
    )(h_a, h_b, h_c, proj_f, proj_f, proj_f, x, mod, wa, wb, wc, wo, norm_w, wr_t, br_t)


def _prefix_count(flags, tri_u, tri_l):
    ne, nr, nl = flags.shape
    fb = flags.astype(BF16)
    inc = _dot(fb.reshape(ne * nr, nl), tri_u).reshape(ne, nr, nl)
    tot = jnp.broadcast_to(inc[:, :, nl - 1:nl], (ne, nr, nl)).astype(BF16)
    offs = jnp.stack([_dot(tri_l, tot[e]) for e in range(ne)], axis=0)
    return offs + inc - flags


def _route_body(lg_ref, slot_ref, rank_ref, aff_ref, *, cap):
    lg = lg_ref[0]
    ne, nr, nl = lg.shape
    ex = jnp.exp(lg - jnp.max(lg, axis=0, keepdims=True))
    aff = ex / jnp.sum(ex, axis=0, keepdims=True)
    bits = pltpu.bitcast(aff, jnp.int32)

    def count(flags):
        return jnp.sum(jnp.sum(flags, axis=1, keepdims=True), axis=2, keepdims=True)

    def bisect(i, thr):
        cand = thr | lax.shift_left(jnp.int32(1), 30 - i)
        cnt = count(jnp.where(bits >= cand, 1.0, 0.0))
        return jnp.where(cnt >= cap, cand, thr)

    thr = lax.fori_loop(0, 31, bisect, jnp.zeros((ne, 1, 1), jnp.int32))
    li = lax.broadcasted_iota(jnp.int32, (nl, nl), 0)
    lj = lax.broadcasted_iota(jnp.int32, (nl, nl), 1)
    tri_u = jnp.where(li <= lj, 1.0, 0.0).astype(BF16)
    ri = lax.broadcasted_iota(jnp.int32, (nr, nr), 0)
    rj = lax.broadcasted_iota(jnp.int32, (nr, nr), 1)
    tri_l = jnp.where(rj < ri, 1.0, 0.0).astype(BF16)
    gt = jnp.where(bits > thr, 1.0, 0.0)
    eq = jnp.where(bits == thr, 1.0, 0.0)
    need = cap - count(gt)
    sel = gt + eq * jnp.where(_prefix_count(eq, tri_u, tri_l) < need, 1.0, 0.0)
    rank = _prefix_count(sel, tri_u, tri_l).astype(jnp.int32)
    rank_ref[0] = rank
    slot_ref[0] = jnp.where(sel > 0.5, rank, -1)
    aff_ref[0] = aff


def _route(logits_t, ng, cap):
    ne, nall = logits_t.shape
    n = nall // ng
    nr = n // LANES
    lg = logits_t.reshape(ne, ng, nr, LANES).transpose(1, 0, 2, 3)
    spec = pl.BlockSpec((1, ne, nr, LANES), lambda g: (g, 0, 0, 0))
    return pl.pallas_call(
        functools.partial(_route_body, cap=cap),
        grid=(ng,),
        in_specs=[spec],
        out_specs=[spec, spec, spec],
        out_shape=[jax.ShapeDtypeStruct((ng, ne, nr, LANES), jnp.int32),
                   jax.ShapeDtypeStruct((ng, ne, nr, LANES), jnp.int32),
                   jax.ShapeDtypeStruct((ng, ne, nr, LANES), F32)],
        compiler_params=_cparams("arbitrary"),
        name="ec_route",
    )(lg)


def _window_count(s0, s1):
    return jnp.where(s1 > s0, (s1 - 1) // LANES - s0 // LANES + 1, 0)


def _expert_body(st_ref, x_ref, slot_ref, aff_ref, wg_ref, wu_ref, wd_ref, ye_ref, xe_ref, gate_ref, *, nb, rt, fc):
    g, e, s = pl.program_id(0), pl.program_id(1), pl.program_id(2)
    ne = pl.num_programs(1)

    @pl.when(s == 0)
    def _():
        xe_ref[...] = jnp.zeros_like(xe_ref)
        gate_ref[...] = jnp.zeros_like(gate_ref)

    @pl.when(s < nb)
    def _():
        base = (g * ne + e) * (nb + 1) + s
        s0, s1 = st_ref[base], st_ref[base + 1]
        w_lo = s0 // LANES
        slot = slot_ref[0, 0]
        aff = aff_ref[0, 0]
        xb = x_ref[0]
        sub = lax.broadcasted_iota(jnp.int32, (LANES, LANES), 0)

        def window(w, carry):
            row0 = pl.multiple_of((w_lo + w) * LANES, LANES)
            tgt = sub + row0
            hits = [slot[c:c + 1, :] == tgt for c in range(slot.shape[0])]
            onehot = jnp.concatenate([jnp.where(h, 1.0, 0.0).astype(BF16) for h in hits], axis=1)
            xe_ref[pl.ds(row0, LANES), :] += _dot(onehot, xb).astype(BF16)
            gsum = functools.reduce(lambda a, b: a + b,
                                    [jnp.where(h, aff[c:c + 1, :], 0.0) for c, h in enumerate(hits)])
            gate_ref[pl.ds(row0, LANES), :] += jnp.sum(gsum, axis=1, keepdims=True)
            return carry

        lax.fori_loop(0, _window_count(s0, s1), window, 0)

    @pl.when(s >= nb)
    def _():
        r = s - nb
        rows = pl.ds(pl.multiple_of(r * rt, rt), rt)
        xe = xe_ref[rows, :]
        f = wg_ref.shape[2]
        acc = None
        for c in range(f // fc):
            fs = slice(c * fc, (c + 1) * fc)
            gg = _dot(xe, wg_ref[0, :, fs])
            hid = (gg * _sigmoid(gg) * _dot(xe, wu_ref[0, :, fs])).astype(BF16)
            part = _dot(hid, wd_ref[0, fs, :])
            acc = part if acc is None else acc + part
        ye_ref[0, 0] = (acc * gate_ref[rows, :]).astype(ye_ref.dtype)


def _expert_ffn(starts, xm, slot, aff, wg, wu, wd, cap, tb, rt):
    ng, n, d = xm.shape
    ne, _, f = wg.shape
    nb = n // tb
    nr = cap // rt
    tok = lambda g, e, s, st: (g, e, jnp.minimum(s, nb - 1), 0)
    grid_spec = pltpu.PrefetchScalarGridSpec(
        num_scalar_prefetch=1,
        grid=(ng, ne, nb + nr),
        in_specs=[pl.BlockSpec((1, tb, d), lambda g, e, s, st: (g, jnp.minimum(s, nb - 1), 0)),
                  pl.BlockSpec((1, 1, tb // LANES, LANES), tok),
                  pl.BlockSpec((1, 1, tb // LANES, LANES), tok),
                  pl.BlockSpec((1, d, f), lambda g, e, s, st: (e, 0, 0)),
                  pl.BlockSpec((1, d, f), lambda g, e, s, st: (e, 0, 0)),
                  pl.BlockSpec((1, f, d), lambda g, e, s, st: (e, 0, 0))],
        out_specs=pl.BlockSpec((1, 1, rt, d), lambda g, e, s, st: (g, e, jnp.maximum(s - nb, 0), 0)),
        scratch_shapes=[pltpu.VMEM((cap, d), BF16), pltpu.VMEM((cap, 1), F32)])
    return pl.pallas_call(
        functools.partial(_expert_body, nb=nb, rt=rt, fc=min(512, f)),
        grid_spec=grid_spec,
        out_shape=jax.ShapeDtypeStruct((ng, ne, cap, d), BF16),
        compiler_params=_cparams("arbitrary", "arbitrary", "arbitrary"),
        name="ec_expert_ffn",
    )(starts, xm, slot, aff, wg, wu, wd)


COMBINE_ALIGN = 16


def _combine_body(st_ref, x_ref, slot_ref, mod_ref, fw_ref, ye_hbm, o_ref, buf_ref, sem_ref, xbuf_ref, xsem_ref,
                  acc_ref, *, nb, tb, ne, span, cap, final):
    g, b = pl.program_id(0), pl.program_id(1)
    step = g * nb + b
    nsteps = pl.num_programs(0) * nb
    par = step % 2

    def bounds(gg, bb, e):
        base = (gg * ne + e) * (nb + 1) + bb
        return st_ref[base], st_ref[base + 1]

    def span_start(gg, bb, e):
        s0, _ = bounds(gg, bb, e)
        return pl.multiple_of(jnp.minimum((s0 // COMBINE_ALIGN) * COMBINE_ALIGN, cap - span), COMBINE_ALIGN)

    def span_copies(gg, bb, parity):
        return [pltpu.make_async_copy(ye_hbm.at[gg, e, pl.ds(span_start(gg, bb, e), span), :],
                                      buf_ref.at[parity, e], sem_ref.at[parity, e]) for e in range(ne)]

    @pl.when(step == 0)
    def _():
        for cp in span_copies(g, b, par):
            cp.start()

    @pl.when(step + 1 < nsteps)
    def _():
        nxt = step + 1
        for cp in span_copies(nxt // nb, nxt % nb, 1 - par):
            cp.start()

    for cp in span_copies(g, b, par):
        cp.wait()

    slot_all = slot_ref[0]
    lane = lax.broadcasted_iota(jnp.int32, (tb, span), 1)
    onehot = jnp.concatenate(
        [jnp.where(slot_all[:, e:e + 1] == span_start(g, b, e) + lane, 1.0, 0.0).astype(BF16) for e in range(ne)],
        axis=1)
    acc_ref[...] = _dot(onehot, buf_ref[par].reshape(ne * span, buf_ref.shape[3]))

    lane_w = lax.broadcasted_iota(jnp.int32, (tb, LANES), 1)
    for e in range(ne):
        _, s1 = bounds(g, b, e)
        done = span_start(g, b, e) + span
        slot_col = slot_all[:, e:e + 1]

        def window(w, carry):
            want = done + w * LANES
            row0 = pl.multiple_of(jnp.minimum(want, cap - LANES), COMBINE_ALIGN)
            cp = pltpu.make_async_copy(ye_hbm.at[g, e, pl.ds(row0, LANES), :], xbuf_ref, xsem_ref.at[0])
            cp.start()
            cp.wait()
            hit = (slot_col == row0 + lane_w) & (slot_col >= want)
            acc_ref[...] += _dot(jnp.where(hit, 1.0, 0.0).astype(BF16), xbuf_ref[...])
            return carry

        lax.fori_loop(0, (jnp.maximum(s1 - done, 0) + LANES - 1) // LANES, window, 0)

    m = mod_ref[0]
    xn = x_ref[...] + m[5:6] * acc_ref[...]
    o_ref[...] = _rms(xn, fw_ref[...]) if final else xn


def _combine(lay, starts, x, slot_t, mod, final_w, ye, tb, final):
    nt, d = x.shape
    ng, n, ne = slot_t.shape
    cap = ye.shape[2]
    nb = n // tb
    span = min(LANES, cap)
    grid_spec = pltpu.PrefetchScalarGridSpec(
        num_scalar_prefetch=1,
        grid=(ng, nb),
        in_specs=[pl.BlockSpec((tb, d), lambda g, b, st: (g * nb + b, 0)),
                  pl.BlockSpec((1, tb, ne), lambda g, b, st: (g, b, 0)),
                  pl.BlockSpec((1, N_MOD, d), lambda g, b, st: (lay.seq_index((g * nb + b) * tb), 0, 0)),
                  pl.BlockSpec((1, d), lambda g, b, st: (0, 0)),
                  pl.BlockSpec(memory_space=pl.ANY)],
        out_specs=pl.BlockSpec((tb, d), lambda g, b, st: (g * nb + b, 0)),
        scratch_shapes=[pltpu.VMEM((2, ne, span, d), BF16), pltpu.SemaphoreType.DMA((2, ne)),
                        pltpu.VMEM((LANES, d), BF16), pltpu.SemaphoreType.DMA((1,)),
                        pltpu.VMEM((tb, d), F32)])
    return pl.pallas_call(
        functools.partial(_combine_body, nb=nb, tb=tb, ne=ne, span=span, cap=cap, final=final),
        grid_spec=grid_spec,
        out_shape=jax.ShapeDtypeStruct((nt, d), F32),
        compiler_params=_cparams("arbitrary", "arbitrary"),
        name="ec_combine",
    )(starts, x, slot_t, mod, final_w, ye)


def _block_starts(rank, tb, cap):
    ng, ne = rank.shape[:2]
    first = rank[:, :, ::tb // LANES, 0]
    return jnp.concatenate([first, jnp.full((ng, ne, 1), cap, jnp.int32)], axis=2).reshape(-1)


def kernel(x_prompt, x_sample, c_prompt, c_sample, norm1_w, norm2_w, w_mod, b_mod, w_in, b_in, mlstm_norm_w, na_rpb, conv_w, conv_b, lru_wa, lru_ba, lru_wx, lru_bx, lru_L, w_br_a, w_br_b, w_br_c, w_out, w_router, b_router, w_gate_e, w_up_e, w_down_e, final_norm_w):
    bp, tp, d = x_prompt.shape
    bs, ts, _ = x_sample.shape
    lay = Layout(bp, tp, bs, ts, d)
    assert lay.n_p == lay.n_s, "the two request groups are routed as equal-sized token sets"
    depth = w_in.shape[0]
    ne = w_router.shape[2]
    ng, n = 2, lay.n_p
    cap = EC_FACTOR * n // ne
    tm_in = min(1024, ts)
    tm_merge = min(512, ts)
    tt_lru = min(256, ts)
    tb_gather = min(1024, n)
    tb_comb = min(256, ts)
    rt = min(512, cap)

    x = jnp.concatenate([x_prompt.reshape(lay.n_p, d), x_sample.reshape(lay.n_s, d)], axis=0)
    nseq = bp + bs
    rows_c = -(-nseq // SUBLANES) * SUBLANES
    c_all = jnp.zeros((rows_c, d), F32).at[:nseq].set(jnp.concatenate([c_prompt, c_sample], axis=0))
    mod_all = _modulation(c_all, w_mod, b_mod).reshape(depth, rows_c, N_MOD, d)

    widths = (d, d, d, d, 4 * MLSTM_HEADS, d, d, d, d, d, N_BRANCH * d)
    offs = [0]
    for w in widths:
        offs.append(offs[-1] + w)
    col = lambda a, i: a[..., offs[i]:offs[i + 1]]
    bcols = (0, 1, 2, 5, 6, 7)
    fcols = (3, 8, 9, 10)

    for l in range(depth):
        wl, bl = w_in[l], b_in[l]
        wb = jnp.concatenate([col(wl, i) for i in bcols], axis=1).astype(BF16)
        bb = jnp.concatenate([col(bl, i) for i in bcols], axis=0).reshape(1, -1)
        wf = jnp.concatenate([col(wl, i) for i in fcols], axis=1).astype(BF16)
        bf = jnp.concatenate([col(bl, i) for i in fcols], axis=0).reshape(1, -1)
        ngate = 4 * MLSTM_HEADS
        wg = jnp.zeros((d, LANES), BF16).at[:, :ngate].set(col(wl, 4).astype(BF16))
        bg = jnp.zeros((1, LANES), F32).at[0, :ngate].set(col(bl, 4))
        mod = mod_all[l]

        proj_b, proj_f, gates = _input_projection(lay, x, mod, norm1_w[l].reshape(1, d), wb, bb, wf, bf, wg, bg, tm_in)
        gates_t = gates[:, :ngate].T

        nw = mlstm_norm_w[l].reshape(1, d)
        h_fwd = _mlstm_pass(lay, proj_b, proj_f, gates, gates_t, nw, None, rev=False)
        h_a = _mlstm_pass(lay, proj_b, proj_f, gates, gates_t, nw, h_fwd, rev=True)

        bias = _na_bias_table(na_rpb[l])
        h_b = jnp.concatenate([_neigh_attn(proj_b, bias, d, 0, bp, tp, 3),
                               _neigh_attn(proj_b, bias, d, lay.n_p, bs, ts, 3)], axis=0)

        lru_args = (conv_w[l], conv_b[l].reshape(1, d))
        l_fwd = _lru_pass(lay, proj_f, *lru_args, _lru_gate_weights(lru_wa[l, 0], lru_wx[l, 0]),
                          lru_ba[l, 0].reshape(1, d), lru_bx[l, 0].reshape(1, d), lru_L[l, 0].reshape(1, d),
                          None, rev=False, tt=tt_lru)
        h_c = _lru_pass(lay, proj_f, *lru_args, _lru_gate_weights(lru_wa[l, 1], lru_wx[l, 1]),
                        lru_ba[l, 1].reshape(1, d), lru_bx[l, 1].reshape(1, d), lru_L[l, 1].reshape(1, d),
                        l_fwd, rev=True, tt=tt_lru)

        x, xm2, logits_t = _merge_project(
            lay, h_a, h_b, h_c, proj_f, x, mod, w_br_a[l].astype(BF16), w_br_b[l].astype(BF16),
            w_br_c[l].astype(BF16), w_out[l].astype(BF16), norm2_w[l].reshape(1, d),
            w_router[l].T.astype(BF16), b_router[l].reshape(ne, 1), tm_merge)

        slot, rank, aff = _route(logits_t, ng, cap)
        ye = _expert_ffn(_block_starts(rank, tb_gather, cap), xm2.reshape(ng, n, d), slot, aff,
                         w_gate_e[l].astype(BF16), w_up_e[l].astype(BF16), w_down_e[l].astype(BF16),
                         cap, tb_gather, rt)
        slot_t = slot.reshape(ng, ne, n).transpose(0, 2, 1)
        x = _combine(lay, _block_starts(rank, tb_comb, cap), x, slot_t, mod, final_norm_w.reshape(1, d), ye,
                     tb_comb, final=(l == depth - 1))

    return (x[:lay.n_p].reshape(bp, tp, d), x[lay.n_p:].reshape(bs, ts, d))
```

```python
import functools
from typing import NamedTuple

import jax
import jax.numpy as jnp
from jax import lax
from jax.experimental import pallas as pl
from jax.experimental.pallas import tpu as pltpu

EPS = 1e-6
NEG_INF = -1e30
GRID_W = 64
MLSTM_HEADS = 4
MLSTM_CHUNK = 128
NA_HEADS = 16
NA_KH = 8
NA_KW = 16
LRU_BLOCKS = 16
LRU_C = 8.0
CONV_W = 4
N_BRANCH = 3
N_EXPERTS = 16
EC_FACTOR = 2
N_MOD = 6

LANES = 128
SUBLANES = 8
MXU_DIM = 256
VMEM_LIMIT = 56 * 1024 * 1024

F32 = jnp.float32
BF16 = jnp.bfloat16


class Layout(NamedTuple):
    bp: int
    tp: int
    bs: int
    ts: int
    d: int

    @property
    def n_p(self):
        return self.bp * self.tp

    @property
    def n_s(self):
        return self.bs * self.ts

    @property
    def nt(self):
        return self.n_p + self.n_s

    def seq_index(self, tok0):
        return jnp.where(tok0 < self.n_p, tok0 // self.tp, self.bp + (tok0 - self.n_p) // self.ts)

    def seq_len(self, tok0):
        return jnp.where(tok0 < self.n_p, self.tp, self.ts)

    def pos_in_seq(self, tok0):
        return jnp.where(tok0 < self.n_p, tok0 % self.tp, (tok0 - self.n_p) % self.ts)


def _cparams(*sem):
    return pltpu.CompilerParams(dimension_semantics=sem, vmem_limit_bytes=VMEM_LIMIT)


def _dot(a, b):
    return jnp.dot(a, b, preferred_element_type=F32)


def _dot_nt(a, b):
    return lax.dot_general(a, b, (((1,), (1,)), ((), ())), preferred_element_type=F32)


def _dot_tn(a, b):
    return lax.dot_general(a, b, (((0,), (0,)), ((), ())), preferred_element_type=F32)


def _sigmoid(x):
    return 1.0 / (1.0 + jnp.exp(-x))


def _rms(x, w):
    return x * lax.rsqrt(jnp.mean(x * x, axis=-1, keepdims=True) + EPS) * w


def _mod_body(c_ref, w_ref, b_ref, o_ref):
    c = c_ref[...]
    ca = (c * _sigmoid(c)).astype(BF16)
    o_ref[0] = _dot(ca, w_ref[0].astype(BF16)) + b_ref[0]


def _modulation(c_all, w_mod, b_mod):
    depth, d, nm = w_mod.shape
    rows = c_all.shape[0]
    tn = 1536
    return pl.pallas_call(
        _mod_body,
        grid=(depth, nm // tn),
        in_specs=[pl.BlockSpec((rows, d), lambda l, j: (0, 0)),
                  pl.BlockSpec((1, d, tn), lambda l, j: (l, 0, j)),
                  pl.BlockSpec((1, 1, tn), lambda l, j: (l, 0, j))],
        out_specs=pl.BlockSpec((1, rows, tn), lambda l, j: (l, 0, j)),
        out_shape=jax.ShapeDtypeStruct((depth, rows, nm), F32),
        compiler_params=_cparams("arbitrary", "arbitrary"),
        name="adaln_modulation",
    )(c_all, w_mod, b_mod.reshape(depth, 1, nm))


def _inproj_body(x_ref, mod_ref, nw_ref, wb_ref, bb_ref, wf_ref, bf_ref, wg_ref, bg_ref,
                 ob_ref, of_ref, og_ref, xm_ref, *, n_bf):
    j = pl.program_id(1)

    @pl.when(j == 0)
    def _():
        m = mod_ref[0]
        xm = (_rms(x_ref[...], nw_ref[...]) * (1.0 + m[1:2]) + m[0:1]).astype(BF16)
        xm_ref[...] = xm
        og_ref[...] = _dot(xm, wg_ref[...]) + bg_ref[...]

    @pl.when(j < n_bf)
    def _():
        ob_ref[...] = (_dot(xm_ref[...], wb_ref[...]) + bb_ref[...]).astype(BF16)

    @pl.when(j >= n_bf)
    def _():
        of_ref[...] = _dot(xm_ref[...], wf_ref[...]) + bf_ref[...]


def _input_projection(lay, x, mod, norm_w, wb, bb, wf, bf, wg, bg, tm, tn):
    nt, d = x.shape
    n_bf, n_f = wb.shape[1] // tn, wf.shape[1] // tn
    seq = lambda i: lay.seq_index(i * tm)
    jb = lambda j: jnp.minimum(j, n_bf - 1)
    jf = lambda j: jnp.maximum(j - n_bf, 0)
    return pl.pallas_call(
        functools.partial(_inproj_body, n_bf=n_bf),
        grid=(nt // tm, n_bf + n_f),
        in_specs=[pl.BlockSpec((tm, d), lambda i, j: (i, 0)),
                  pl.BlockSpec((1, N_MOD, d), lambda i, j: (seq(i), 0, 0)),
                  pl.BlockSpec((1, d), lambda i, j: (0, 0)),
                  pl.BlockSpec((d, tn), lambda i, j: (0, jb(j))),
                  pl.BlockSpec((1, tn), lambda i, j: (0, jb(j))),
                  pl.BlockSpec((d, tn), lambda i, j: (0, jf(j))),
                  pl.BlockSpec((1, tn), lambda i, j: (0, jf(j))),
                  pl.BlockSpec((d, LANES), lambda i, j: (0, 0)),
                  pl.BlockSpec((1, LANES), lambda i, j: (0, 0))],
        out_specs=[pl.BlockSpec((tm, tn), lambda i, j: (i, jb(j))),
                   pl.BlockSpec((tm, tn), lambda i, j: (i, jf(j))),
                   pl.BlockSpec((tm, LANES), lambda i, j: (i, 0))],
        out_shape=[jax.ShapeDtypeStruct((nt, wb.shape[1]), BF16),
                   jax.ShapeDtypeStruct((nt, wf.shape[1]), F32),
                   jax.ShapeDtypeStruct((nt, LANES), F32)],
        scratch_shapes=[pltpu.VMEM((tm, d), BF16)],
        compiler_params=_cparams("arbitrary", "arbitrary"),
        name="norm_inproj",
    )(x, mod, norm_w, wb, bb, wf, bf, wg, bg)


def _lane_cumsum(x, rev):
    n = x.shape[-1]
    lane = lax.broadcasted_iota(jnp.int32, x.shape, 1)
    k = 1
    while k < n:
        if rev:
            x = x + jnp.where(lane < n - k, pltpu.roll(x, n - k, 1), 0.0)
        else:
            x = x + jnp.where(lane >= k, pltpu.roll(x, k, 1), 0.0)
        k *= 2
    return x


def _log_sigmoid(x):
    return jnp.minimum(x, 0.0) - jnp.log(1.0 + jnp.exp(-jnp.abs(x)))


def _mlstm_body(qf_ref, kf_ref, vf_ref, grf_ref, gcf_ref, qb_ref, kb_ref, vb_ref, grb_ref, gcb_ref,
                hf_ref, hb_ref, c_ref, m_ref, wi_ref, col_ref, sc_ref, qc_ref, *, lay, heads, chunk):
    s = pl.program_id(0)
    tok_f = s * chunk
    tok_b = (pl.num_programs(0) - 1 - s) * chunk
    dh = qf_ref.shape[1] // heads

    def reset(lo):
        c_ref[lo:lo + heads] = jnp.zeros((heads,) + c_ref.shape[1:], F32)
        m_ref[lo:lo + heads] = jnp.zeros((heads,) + m_ref.shape[1:], F32)

    @pl.when(lay.pos_in_seq(tok_f) == 0)
    def _():
        reset(0)

    @pl.when(lay.pos_in_seq(tok_b) + chunk == lay.seq_len(tok_b))
    def _():
        reset(heads)

    sides = ((qf_ref, kf_ref, vf_ref, grf_ref, gcf_ref, hf_ref), (qb_ref, kb_ref, vb_ref, grb_ref, gcb_ref, hb_ref))
    units = [(rev, h) for rev in (0, 1) for h in range(heads)]
    ri = lax.broadcasted_iota(jnp.int32, (chunk, chunk), 0)
    cj = lax.broadcasted_iota(jnp.int32, (chunk, chunk), 1)
    kscale = dh ** -0.5
    ones = jnp.ones((chunk, LANES), BF16)
    rep = lambda col: jnp.broadcast_to(col, (chunk, LANES))

    def scaled_k(rev, h):
        return (sides[rev][1][:, h * dh:(h + 1) * dh].astype(F32) * kscale).astype(BF16)

    decays, m_news = [], []
    for rev in (0, 1):
        gr = sides[rev][3][...]
        gc = sides[rev][4][...]
        gi = 2 * heads * rev
        lf_all = _log_sigmoid(gr)
        b_all = _lane_cumsum(lf_all, bool(rev))
        mask = (cj >= ri) if rev else (cj <= ri)
        for h in range(heads):
            u = rev * heads + h
            ig_row = gr[gi + h:gi + h + 1, :]
            lf_row = lf_all[gi + heads + h:gi + heads + h + 1, :]
            b_row = b_all[gi + heads + h:gi + heads + h + 1, :]
            ig_col = gc[:, gi + h:gi + h + 1]
            m_old = m_ref[u][:, 0:1]
            b_col = jnp.sum(jnp.where(mask, lf_row, 0.0), axis=1, keepdims=True)
            dmat = jnp.where(mask, b_col - b_row + ig_row, NEG_INF)
            g = b_col + m_old
            m_row = jnp.maximum(g, jnp.max(dmat, axis=1, keepdims=True))
            wi_ref[u] = jnp.exp(dmat - m_row)
            b_last = b_row[:, 0:1] if rev else b_row[:, chunk - 1:chunk]
            m_new = jnp.maximum(b_last + m_old, jnp.max(b_last - b_row + ig_row, axis=1, keepdims=True))
            col_ref[u, 0] = rep(jnp.exp(g - m_row))
            col_ref[u, 1] = rep(jnp.exp(-m_row))
            col_ref[u, 2] = rep(jnp.exp(b_last - b_col + ig_col - m_new))
            decays.append(jnp.exp(b_last + m_old - m_new))
            m_news.append(m_new)

    for rev, h in units:
        u = rev * heads + h
        q = sides[rev][0][:, h * dh:(h + 1) * dh]
        sc_ref[u] = (_dot_nt(q, scaled_k(rev, h)) * wi_ref[u]).astype(BF16)
    for rev, h in units:
        u = rev * heads + h
        q = sides[rev][0][:, h * dh:(h + 1) * dh]
        qc_ref[u] = _dot(q, c_ref[u].astype(BF16))

    for rev, h in units:
        u = rev * heads + h
        sl = slice(h * dh, (h + 1) * dh)
        v_ext = jnp.concatenate([sides[rev][2][:, sl], ones], axis=1)
        w_inter = col_ref[u, 0]
        ne = _dot(sc_ref[u], v_ext) + jnp.concatenate([w_inter] * (dh // LANES + 1), axis=1) * qc_ref[u]
        den = jnp.maximum(jnp.abs(ne[:, dh:]), col_ref[u, 1])
        sides[rev][5][:, sl] = ne[:, :dh] / jnp.concatenate([den] * (dh // LANES), axis=1)

    for rev, h in units:
        u = rev * heads + h
        sl = slice(h * dh, (h + 1) * dh)
        v_ext = jnp.concatenate([sides[rev][2][:, sl], ones], axis=1)
        kw = scaled_k(rev, h).astype(F32) * jnp.concatenate([col_ref[u, 2]] * (dh // LANES), axis=1)
        c_ref[u] = decays[u] * c_ref[u] + _dot_tn(kw.astype(BF16), v_ext)
        m_ref[u] = jnp.broadcast_to(m_news[u], m_ref.shape[1:])


def _mlstm(lay, proj_b, gates, gates_t):
    nt, d = lay.nt, lay.d
    chunk, heads = MLSTM_CHUNK, MLSTM_HEADS
    nc = nt // chunk
    dh = d // heads
    nu = 2 * heads

    def side(cidx):
        return [pl.BlockSpec((chunk, d), lambda s: (cidx(s), 0)),
                pl.BlockSpec((chunk, d), lambda s: (cidx(s), 1)),
                pl.BlockSpec((chunk, d), lambda s: (cidx(s), 2)),
                pl.BlockSpec((4 * heads, chunk), lambda s: (0, cidx(s))),
                pl.BlockSpec((chunk, LANES), lambda s: (cidx(s), 0))]

    fwd = lambda s: s
    bwd = lambda s: nc - 1 - s
    args = [proj_b, proj_b, proj_b, gates_t, gates]
    return pl.pallas_call(
        functools.partial(_mlstm_body, lay=lay, heads=heads, chunk=chunk),
        grid=(nc,),
        in_specs=side(fwd) + side(bwd),
        out_specs=[pl.BlockSpec((chunk, d), lambda s: (fwd(s), 0)), pl.BlockSpec((chunk, d), lambda s: (bwd(s), 0))],
        out_shape=[jax.ShapeDtypeStruct((nt, d), F32), jax.ShapeDtypeStruct((nt, d), F32)],
        scratch_shapes=[pltpu.VMEM((nu, dh, dh + LANES), F32), pltpu.VMEM((nu, 1, LANES), F32),
                        pltpu.VMEM((nu, chunk, chunk), F32), pltpu.VMEM((nu, 3, chunk, LANES), F32),
                        pltpu.VMEM((nu, chunk, chunk), BF16), pltpu.VMEM((nu, chunk, dh + LANES), F32)],
        compiler_params=_cparams("arbitrary"),
        name="mlstm_bidir",
    )(*(args + args))


def _na_bias_table(rpb):
    h = rpb.shape[0]
    cols = jnp.arange(GRID_W)
    cs = jnp.clip(cols - NA_KW // 2, 0, GRID_W - NA_KW)
    col_in = (cols[None, :] >= cs[:, None]) & (cols[None, :] < cs[:, None] + NA_KW)
    dc_idx = jnp.clip(cols[None, :] - cols[:, None], -(NA_KW - 1), NA_KW - 1) + NA_KW - 1
    t = jnp.where(col_in[None, None], rpb[:, :, dc_idx], NEG_INF)
    d_idx = jnp.arange(NA_KH)[:, None] + jnp.arange(NA_KH)[None, :]
    tc = t[:, d_idx]
    tc = tc.transpose(0, 1, 3, 2, 4).reshape(h, NA_KH, GRID_W, NA_KH * GRID_W)
    return tc.reshape(h // 2, 2, NA_KH, GRID_W, NA_KH * GRID_W).transpose(0, 2, 1, 3, 4).reshape(
        h // 2, NA_KH, 2 * GRID_W, NA_KH * GRID_W).astype(F32)


def _na_body(q_ref, k_ref, v_ref, bias_ref, o_ref, s_ref, p_ref, l_ref, *, rows, rblk, dh):
    rb = pl.program_id(2)
    lane_q = lax.broadcasted_iota(jnp.int32, (GRID_W, 2 * dh), 1)
    qscale = dh ** -0.5

    starts, scores = [], []
    for r in range(rblk):
        rg = rb * rblk + r
        rs = jnp.clip(rg - NA_KH // 2, 0, rows - NA_KH)
        d0 = rs - rg + NA_KH - 1
        q = q_ref[r * GRID_W:(r + 1) * GRID_W, :].astype(F32) * qscale
        q2 = jnp.concatenate([jnp.where(lane_q < dh, q, 0.0), jnp.where(lane_q >= dh, q, 0.0)], axis=0).astype(BF16)
        k0 = pl.multiple_of(rs * GRID_W, GRID_W)
        starts.append(k0)
        s_ref[r] = _dot_nt(q2, k_ref[pl.ds(k0, NA_KH * GRID_W), :]) + bias_ref[0, d0]
    for r in range(rblk):
        s = s_ref[r]
        p = jnp.exp(s - jnp.max(s, axis=1, keepdims=True))
        l_ref[r] = jnp.broadcast_to(jnp.sum(p, axis=1, keepdims=True), l_ref.shape[1:])
        p_ref[r] = p.astype(BF16)
    for r in range(rblk):
        o = _dot(p_ref[r], v_ref[pl.ds(starts[r], NA_KH * GRID_W), :]) / l_ref[r]
        out = jnp.where(lane_q < dh, o[:GRID_W], o[GRID_W:])
        o_ref[r * GRID_W:(r + 1) * GRID_W, :] = out.astype(o_ref.dtype)


def _neigh_attn(proj_b, bias, d, tok_off, nseq, t, col0):
    rows = t // GRID_W
    rblk = 8
    dh = d // NA_HEADS
    pairs = NA_HEADS // 2
    pw = 2 * dh
    per = d // pw
    seq0 = tok_off // t
    qblk = rblk * GRID_W
    q0 = tok_off // qblk
    nrb = rows // rblk
    return pl.pallas_call(
        functools.partial(_na_body, rows=rows, rblk=rblk, dh=dh),
        grid=(pairs, nseq, nrb),
        in_specs=[pl.BlockSpec((qblk, pw), lambda p, b, r: (q0 + b * nrb + r, col0 * per + p)),
                  pl.BlockSpec((t, pw), lambda p, b, r: (seq0 + b, (col0 + 1) * per + p)),
                  pl.BlockSpec((t, pw), lambda p, b, r: (seq0 + b, (col0 + 2) * per + p)),
                  pl.BlockSpec((1, NA_KH, 2 * GRID_W, NA_KH * GRID_W), lambda p, b, r: (p, 0, 0, 0))],
        out_specs=pl.BlockSpec((qblk, pw), lambda p, b, r: (b * nrb + r, p)),
        out_shape=jax.ShapeDtypeStruct((nseq * t, d), BF16),
        scratch_shapes=[pltpu.VMEM((rblk, 2 * GRID_W, NA_KH * GRID_W), F32),
                        pltpu.VMEM((rblk, 2 * GRID_W, NA_KH * GRID_W), BF16),
                        pltpu.VMEM((rblk, 2 * GRID_W, LANES), F32)],
        compiler_params=_cparams("arbitrary", "arbitrary", "arbitrary"),
        name="neigh_attn",
    )(proj_b, proj_b, proj_b, bias)


def _softplus(x):
    return jnp.maximum(x, 0.0) + jnp.log(1.0 + jnp.exp(-jnp.abs(x)))


def _gelu_tanh(x):
    return 0.5 * x * (1.0 + jnp.tanh(0.7978845608028654 * (x + 0.044715 * (x * x * x))))


def _lru_body(*refs, lay, rev, tt):
    if rev:
        (x_ref, xp_ref, xn_ref, cw_ref, cb_ref, w_ref, ba_ref, bx_ref, lam_ref, hf_ref, y_ref,
         out_ref, carry_ref) = refs
    else:
        x_ref, xp_ref, xn_ref, cw_ref, cb_ref, w_ref, ba_ref, bx_ref, lam_ref, out_ref, carry_ref = refs
    s = pl.program_id(0)
    ti = (pl.num_programs(0) - 1 - s) if rev else s
    tok0 = ti * tt
    pos = lay.pos_in_seq(tok0)
    slen = lay.seq_len(tok0)
    is_first = pos == 0
    is_last = pos + tt == slen

    @pl.when(is_last if rev else is_first)
    def _():
        carry_ref[...] = jnp.zeros_like(carry_ref)

    d = x_ref.shape[1]
    x = x_ref[...]
    halo = SUBLANES
    xprev = jnp.where(is_first, 0.0, xp_ref[...])
    xnext = jnp.where(is_last, 0.0, xn_ref[...])
    xe = jnp.concatenate([xprev, x, xnext], axis=0)
    pad = CONV_W // 2
    xc = cb_ref[...]
    for j in range(CONV_W):
        o = halo - pad + j
        xc = xc + xe[o:o + tt] * cw_ref[j:j + 1, :]
    sp = _softplus(-lam_ref[...])
    ng = tt // SUBLANES
    sub = lax.broadcasted_iota(jnp.int32, (ng, SUBLANES, MXU_DIM), 1)
    for t in range(d // MXU_DIM):
        cs = slice(t * MXU_DIM, (t + 1) * MXU_DIM)
        xct = xc[:, cs]
        gg = _dot(xct.astype(BF16), w_ref[t])
        r = _sigmoid(gg[:, :MXU_DIM] + ba_ref[:, cs])
        i = _sigmoid(gg[:, MXU_DIM:] + bx_ref[:, cs])
        log_a = -LRU_C * r * sp[:, cs]
        a = jnp.exp(log_a)
        bt = jnp.sqrt(1.0 - a * a) * (i * xct)
        a3 = a.reshape(ng, SUBLANES, MXU_DIM)
        b3 = bt.reshape(ng, SUBLANES, MXU_DIM)
        k = 1
        while k < SUBLANES:
            if rev:
                ok = sub < SUBLANES - k
                ash = pltpu.roll(a3, SUBLANES - k, 1)
                bsh = pltpu.roll(b3, SUBLANES - k, 1)
            else:
                ok = sub >= k
                ash = pltpu.roll(a3, k, 1)
                bsh = pltpu.roll(b3, k, 1)
            b3 = jnp.where(ok, a3 * bsh + b3, b3)
            a3 = jnp.where(ok, a3 * ash, a3)
            k *= 2
        hprev = carry_ref[:, cs]
        outs = [None] * ng
        order = range(ng - 1, -1, -1) if rev else range(ng)
        for gidx in order:
            hg = a3[gidx] * hprev + b3[gidx]
            outs[gidx] = hg
            hprev = hg[0:1] if rev else hg[SUBLANES - 1:SUBLANES]
        carry_ref[:, cs] = hprev
        hh = jnp.concatenate(outs, axis=0)
        if rev:
            out_ref[:, cs] = ((hf_ref[:, cs] + hh) * _gelu_tanh(y_ref[:, cs])).astype(out_ref.dtype)
        else:
            out_ref[:, cs] = hh


def _lru_pass(lay, proj_f, conv_w, conv_b, w_dir, ba, bx, lam, h_fwd, rev, tt):
    nt, d = lay.nt, lay.d
    ntile = nt // tt
    hb = tt // SUBLANES
    nhb = nt // SUBLANES
    tidx = (lambda s: ntile - 1 - s) if rev else (lambda s: s)
    xcol = 1
    in_specs = [pl.BlockSpec((tt, d), lambda s: (tidx(s), xcol)),
                pl.BlockSpec((SUBLANES, d), lambda s: (jnp.maximum(tidx(s) * hb - 1, 0), xcol)),
                pl.BlockSpec((SUBLANES, d), lambda s: (jnp.minimum((tidx(s) + 1) * hb, nhb - 1), xcol)),
                pl.BlockSpec((CONV_W, d), lambda s: (0, 0)),
                pl.BlockSpec((1, d), lambda s: (0, 0)),
                pl.BlockSpec((d // MXU_DIM, MXU_DIM, 2 * MXU_DIM), lambda s: (0, 0, 0)),
                pl.BlockSpec((1, d), lambda s: (0, 0)),
                pl.BlockSpec((1, d), lambda s: (0, 0)),
                pl.BlockSpec((1, d), lambda s: (0, 0))]
    args = [proj_f, proj_f, proj_f, conv_w, conv_b, w_dir, ba, bx, lam]
    if rev:
        in_specs += [pl.BlockSpec((tt, d), lambda s: (tidx(s), 0)),
                     pl.BlockSpec((tt, d), lambda s: (tidx(s), 2))]
        args += [h_fwd, proj_f]
    return pl.pallas_call(
        functools.partial(_lru_body, lay=lay, rev=rev, tt=tt),
        grid=(ntile,),
        in_specs=in_specs,
        out_specs=pl.BlockSpec((tt, d), lambda s: (tidx(s), 0)),
        out_shape=jax.ShapeDtypeStruct((nt, d), BF16 if rev else F32),
        scratch_shapes=[pltpu.VMEM((1, d), F32)],
        compiler_params=_cparams("arbitrary"),
        name="rglru_bwd" if rev else "rglru_fwd",
    )(*args)


def _lru_gate_weights(wa, wx):
    nb, bw, _ = wa.shape
    per = MXU_DIM // bw
    nt = nb // per

    def diag(w):
        w = w.reshape(nt, per, bw, bw)
        eye = jnp.eye(per, dtype=w.dtype)
        return jnp.einsum('tpkj,pq->tpkqj', w, eye).reshape(nt, MXU_DIM, MXU_DIM)

    return jnp.concatenate([diag(wa), diag(wx)], axis=-1).astype(BF16)


def _merge_body(hf_ref, hr_ref, o_ref, hnw_ref, hbp_ref, hbs_ref, hc_ref, g0_ref, g1_ref, g2_ref, x_ref, mod_ref,
                wa_ref, wb_ref, wc_ref, wo_ref, nw_ref, wr_ref, br_ref, xo_ref, xm_ref, lg_ref, *, heads, np_tiles):
    m = mod_ref[0]
    h_b = jnp.where(pl.program_id(0) < np_tiles, hbp_ref[...], hbs_ref[...])
    dh = hf_ref.shape[1] // heads
    parts = []
    for h in range(heads):
        sl = slice(h * dh, (h + 1) * dh)
        hs = hf_ref[:, sl] + hr_ref[:, sl]
        mu = jnp.mean(hs, axis=1, keepdims=True)
        var = jnp.mean(jnp.square(hs - mu), axis=1, keepdims=True)
        hn = (hs - mu) * lax.rsqrt(var + EPS) * hnw_ref[:, sl]
        parts.append((_sigmoid(o_ref[:, sl]) * hn).astype(BF16))
    h_a = jnp.concatenate(parts, axis=1)
    merged = (_sigmoid(g0_ref[...]) * _dot(h_a, wa_ref[...])
              + _sigmoid(g1_ref[...]) * _dot(h_b, wb_ref[...])
              + _sigmoid(g2_ref[...]) * _dot(hc_ref[...], wc_ref[...]))
    xn = x_ref[...] + m[2:3] * _dot(merged.astype(BF16), wo_ref[...])
    xo_ref[...] = xn
    xm = (_rms(xn, nw_ref[...]) * (1.0 + m[4:5]) + m[3:4]).astype(BF16)
    xm_ref[...] = xm
    lg_ref[...] = _dot_nt(wr_ref[...], xm) + br_ref[...]


def _merge_project(lay, h_fwd, h_rev, mlstm_nw, hb_p, hb_s, h_c, proj_f, x, mod, wa, wb, wc, wo, norm_w, wr_t, br_t,
                   tm):
    nt, d = x.shape
    ne = wr_t.shape[0]
    seq = lambda i: lay.seq_index(i * tm)
    act = lambda c: pl.BlockSpec((tm, d), lambda i: (i, c))
    const = lambda shape: pl.BlockSpec(shape, lambda i: (0,) * len(shape), pipeline_mode=pl.Buffered(1))
    np_tiles = lay.n_p // tm
    return pl.pallas_call(
        functools.partial(_merge_body, heads=MLSTM_HEADS, np_tiles=np_tiles),
        grid=(nt // tm,),
        in_specs=[act(0), act(0), act(0), const((1, d)),
                  pl.BlockSpec((tm, d), lambda i: (jnp.minimum(i, np_tiles - 1), 0)),
                  pl.BlockSpec((tm, d), lambda i: (jnp.maximum(i - np_tiles, 0), 0)),
                  act(0), act(3), act(4), act(5), act(0),
                  pl.BlockSpec((1, N_MOD, d), lambda i: (seq(i), 0, 0)),
                  const((d, d)), const((d, d)), const((d, d)), const((d, d)),
                  const((1, d)), const((ne, d)), const((ne, 1))],
        out_specs=[pl.BlockSpec((tm, d), lambda i: (i, 0)),
                   pl.BlockSpec((tm, d), lambda i: (i, 0)),
                   pl.BlockSpec((ne, tm), lambda i: (0, i))],
        out_shape=[jax.ShapeDtypeStruct((nt, d), F32),
                   jax.ShapeDtypeStruct((nt, d), BF16),
                   jax.ShapeDtypeStruct((ne, nt), F32)],
        compiler_params=_cparams("arbitrary"),
        name="merge_outproj",
    )(h_fwd, h_rev, proj_f, mlstm_nw, hb_p, hb_s, h_c, proj_f, proj_f, proj_f, x, mod, wa, wb, wc, wo, norm_w, wr_t, br_t)


def _prefix_count(flags, tri_u, tri_l):
    ne, nr, nl = flags.shape
    fb = flags.astype(BF16)
    inc = _dot(fb.reshape(ne * nr, nl), tri_u).reshape(ne, nr, nl)
    tot = jnp.broadcast_to(inc[:, :, nl - 1:nl], (ne, nr, nl)).astype(BF16)
    offs = jnp.stack([_dot(tri_l, tot[e]) for e in range(ne)], axis=0)
    return offs + inc - flags


def _route_body(lg_ref, slot_ref, rank_ref, aff_ref, *, cap):
    lg = lg_ref[0]
    ne, nr, nl = lg.shape
    ex = jnp.exp(lg - jnp.max(lg, axis=0, keepdims=True))
    aff = ex / jnp.sum(ex, axis=0, keepdims=True)
    bits = pltpu.bitcast(aff, jnp.int32)

    def count(flags):
        return jnp.sum(jnp.sum(flags, axis=1, keepdims=True), axis=2, keepdims=True)

    def bisect(i, thr):
        cand = thr | lax.shift_left(jnp.int32(1), 30 - i)
        cnt = count(jnp.where(bits >= cand, 1.0, 0.0))
        return jnp.where(cnt >= cap, cand, thr)

    thr = lax.fori_loop(0, 31, bisect, jnp.zeros((ne, 1, 1), jnp.int32))
    li = lax.broadcasted_iota(jnp.int32, (nl, nl), 0)
    lj = lax.broadcasted_iota(jnp.int32, (nl, nl), 1)
    tri_u = jnp.where(li <= lj, 1.0, 0.0).astype(BF16)
    ri = lax.broadcasted_iota(jnp.int32, (nr, nr), 0)
    rj = lax.broadcasted_iota(jnp.int32, (nr, nr), 1)
    tri_l = jnp.where(rj < ri, 1.0, 0.0).astype(BF16)
    gt = jnp.where(bits > thr, 1.0, 0.0)
    eq = jnp.where(bits == thr, 1.0, 0.0)
    need = cap - count(gt)
    sel = gt + eq * jnp.where(_prefix_count(eq, tri_u, tri_l) < need, 1.0, 0.0)
    rank = _prefix_count(sel, tri_u, tri_l).astype(jnp.int32)
    rank_ref[0] = rank
    slot_ref[0] = jnp.where(sel > 0.5, rank, -1)
    aff_ref[0] = aff


def _route(logits_t, ng, cap):
    ne, nall = logits_t.shape
    n = nall // ng
    nr = n // LANES
    lg = logits_t.reshape(ne, ng, nr, LANES).transpose(1, 0, 2, 3)
    spec = pl.BlockSpec((1, ne, nr, LANES), lambda g: (g, 0, 0, 0))
    return pl.pallas_call(
        functools.partial(_route_body, cap=cap),
        grid=(ng,),
        in_specs=[spec],
        out_specs=[spec, spec, spec],
        out_shape=[jax.ShapeDtypeStruct((ng, ne, nr, LANES), jnp.int32),
                   jax.ShapeDtypeStruct((ng, ne, nr, LANES), jnp.int32),
                   jax.ShapeDtypeStruct((ng, ne, nr, LANES), F32)],
        compiler_params=_cparams("arbitrary"),
        name="ec_route",
    )(lg)


def _window_count(s0, s1):
    return jnp.where(s1 > s0, (s1 - 1) // LANES - s0 // LANES + 1, 0)


def _expert_body(st_ref, x_ref, slot_ref, aff_ref, wg_ref, wu_ref, wd_ref, ye_ref, xe_ref, gate_ref, *, nb, rt, fc):
    g, e, s = pl.program_id(0), pl.program_id(1), pl.program_id(2)
    ne = pl.num_programs(1)

    @pl.when(s == 0)
    def _():
        xe_ref[...] = jnp.zeros_like(xe_ref)
        gate_ref[...] = jnp.zeros_like(gate_ref)

    @pl.when(s < nb)
    def _():
        base = (g * ne + e) * (nb + 1) + s
        s0, s1 = st_ref[base], st_ref[base + 1]
        w_lo = s0 // LANES
        slot = slot_ref[0, 0]
        aff = aff_ref[0, 0]
        xb = x_ref[0]
        sub = lax.broadcasted_iota(jnp.int32, (LANES, LANES), 0)

        def window(w, carry):
            row0 = pl.multiple_of((w_lo + w) * LANES, LANES)
            tgt = sub + row0
            hits = [slot[c:c + 1, :] == tgt for c in range(slot.shape[0])]
            onehot = jnp.concatenate([jnp.where(h, 1.0, 0.0).astype(BF16) for h in hits], axis=1)
            xe_ref[pl.ds(row0, LANES), :] += _dot(onehot, xb).astype(BF16)
            gsum = functools.reduce(lambda a, b: a + b,
                                    [jnp.where(h, aff[c:c + 1, :], 0.0) for c, h in enumerate(hits)])
            gate_ref[pl.ds(row0, LANES), :] += jnp.sum(gsum, axis=1, keepdims=True)
            return carry

        lax.fori_loop(0, _window_count(s0, s1), window, 0)

    @pl.when(s >= nb)
    def _():
        r = s - nb
        rows = pl.ds(pl.multiple_of(r * rt, rt), rt)
        xe = xe_ref[rows, :]
        f = wg_ref.shape[2]
        acc = None
        for c in range(f // fc):
            fs = slice(c * fc, (c + 1) * fc)
            gg = _dot(xe, wg_ref[0, :, fs])
            hid = (gg * _sigmoid(gg) * _dot(xe, wu_ref[0, :, fs])).astype(BF16)
            part = _dot(hid, wd_ref[0, fs, :])
            acc = part if acc is None else acc + part
        ye_ref[0, 0] = (acc * gate_ref[rows, :]).astype(ye_ref.dtype)


def _expert_ffn(starts, xm, slot, aff, wg, wu, wd, cap, tb, rt):
    ng, n, d = xm.shape
    ne, _, f = wg.shape
    nb = n // tb
    nr = cap // rt
    tok = lambda g, e, s, st: (g, e, jnp.minimum(s, nb - 1), 0)
    grid_spec = pltpu.PrefetchScalarGridSpec(
        num_scalar_prefetch=1,
        grid=(ng, ne, nb + nr),
        in_specs=[pl.BlockSpec((1, tb, d), lambda g, e, s, st: (g, jnp.minimum(s, nb - 1), 0)),
                  pl.BlockSpec((1, 1, tb // LANES, LANES), tok),
                  pl.BlockSpec((1, 1, tb // LANES, LANES), tok),
                  pl.BlockSpec((1, d, f), lambda g, e, s, st: (e, 0, 0)),
                  pl.BlockSpec((1, d, f), lambda g, e, s, st: (e, 0, 0)),
                  pl.BlockSpec((1, f, d), lambda g, e, s, st: (e, 0, 0))],
        out_specs=pl.BlockSpec((1, 1, rt, d), lambda g, e, s, st: (g, e, jnp.maximum(s - nb, 0), 0)),
        scratch_shapes=[pltpu.VMEM((cap, d), BF16), pltpu.VMEM((cap, 1), F32)])
    return pl.pallas_call(
        functools.partial(_expert_body, nb=nb, rt=rt, fc=min(512, f)),
        grid_spec=grid_spec,
        out_shape=jax.ShapeDtypeStruct((ng, ne, cap, d), BF16),
        compiler_params=_cparams("arbitrary", "arbitrary", "arbitrary"),
        name="ec_expert_ffn",
    )(starts, xm, slot, aff, wg, wu, wd)


COMBINE_ALIGN = 16


def _combine_body(st_ref, x_ref, slot_ref, mod_ref, fw_ref, ye_hbm, *rest, nb, tb, ne, span, cap, final):
    o_refs = rest[:-5]
    buf_ref, sem_ref, xbuf_ref, xsem_ref, acc_ref = rest[-5:]
    g, b = pl.program_id(0), pl.program_id(1)
    step = g * nb + b
    nsteps = pl.num_programs(0) * nb
    par = step % 2

    def bounds(gg, bb, e):
        base = (gg * ne + e) * (nb + 1) + bb
        return st_ref[base], st_ref[base + 1]

    def span_start(gg, bb, e):
        s0, _ = bounds(gg, bb, e)
        return pl.multiple_of(jnp.minimum((s0 // COMBINE_ALIGN) * COMBINE_ALIGN, cap - span), COMBINE_ALIGN)

    def span_copies(gg, bb, parity):
        return [pltpu.make_async_copy(ye_hbm.at[gg, e, pl.ds(span_start(gg, bb, e), span), :],
                                      buf_ref.at[parity, e], sem_ref.at[parity, e]) for e in range(ne)]

    @pl.when(step == 0)
    def _():
        for cp in span_copies(g, b, par):
            cp.start()

    @pl.when(step + 1 < nsteps)
    def _():
        nxt = step + 1
        for cp in span_copies(nxt // nb, nxt % nb, 1 - par):
            cp.start()

    for cp in span_copies(g, b, par):
        cp.wait()

    slot_all = slot_ref[0]
    lane = lax.broadcasted_iota(jnp.int32, (tb, span), 1)
    onehot = jnp.concatenate(
        [jnp.where(slot_all[:, e:e + 1] == span_start(g, b, e) + lane, 1.0, 0.0).astype(BF16) for e in range(ne)],
        axis=1)
    acc_ref[...] = _dot(onehot, buf_ref[par].reshape(ne * span, buf_ref.shape[3]))

    lane_w = lax.broadcasted_iota(jnp.int32, (tb, LANES), 1)
    for e in range(ne):
        _, s1 = bounds(g, b, e)
        done = span_start(g, b, e) + span
        slot_col = slot_all[:, e:e + 1]

        def window(w, carry):
            want = done + w * LANES
            row0 = pl.multiple_of(jnp.minimum(want, cap - LANES), COMBINE_ALIGN)
            cp = pltpu.make_async_copy(ye_hbm.at[g, e, pl.ds(row0, LANES), :], xbuf_ref, xsem_ref.at[0])
            cp.start()
            cp.wait()
            hit = (slot_col == row0 + lane_w) & (slot_col >= want)
            acc_ref[...] += _dot(jnp.where(hit, 1.0, 0.0).astype(BF16), xbuf_ref[...])
            return carry

        lax.fori_loop(0, (jnp.maximum(s1 - done, 0) + LANES - 1) // LANES, window, 0)

    m = mod_ref[0]
    xn = x_ref[...] + m[5:6] * acc_ref[...]
    if final:
        y = _rms(xn, fw_ref[...])
        for gi, ref in enumerate(o_refs):

            @pl.when(g == gi)
            def _():
                ref[...] = y
    else:
        o_refs[0][...] = xn


def _combine(lay, starts, x, slot_t, mod, final_w, ye, tb, final):
    nt, d = x.shape
    ng, n, ne = slot_t.shape
    cap = ye.shape[2]
    nb = n // tb
    span = min(LANES, cap)
    if final:
        out_specs = [pl.BlockSpec((tb, d),
                                  lambda g, b, st, gi=gi: (jnp.where(g == gi, b, jnp.where(g > gi, nb - 1, 0)), 0))
                     for gi in range(ng)]
        out_shape = [jax.ShapeDtypeStruct((n, d), F32) for _ in range(ng)]
    else:
        out_specs = [pl.BlockSpec((tb, d), lambda g, b, st: (g * nb + b, 0))]
        out_shape = [jax.ShapeDtypeStruct((nt, d), F32)]
    grid_spec = pltpu.PrefetchScalarGridSpec(
        num_scalar_prefetch=1,
        grid=(ng, nb),
        in_specs=[pl.BlockSpec((tb, d), lambda g, b, st: (g * nb + b, 0)),
                  pl.BlockSpec((1, tb, ne), lambda g, b, st: (g, b, 0)),
                  pl.BlockSpec((1, N_MOD, d), lambda g, b, st: (lay.seq_index((g * nb + b) * tb), 0, 0)),
                  pl.BlockSpec((1, d), lambda g, b, st: (0, 0)),
                  pl.BlockSpec(memory_space=pl.ANY)],
        out_specs=out_specs,
        scratch_shapes=[pltpu.VMEM((2, ne, span, d), BF16), pltpu.SemaphoreType.DMA((2, ne)),
                        pltpu.VMEM((LANES, d), BF16), pltpu.SemaphoreType.DMA((1,)),
                        pltpu.VMEM((tb, d), F32)])
    return pl.pallas_call(
        functools.partial(_combine_body, nb=nb, tb=tb, ne=ne, span=span, cap=cap, final=final),
        grid_spec=grid_spec,
        out_shape=out_shape,
        compiler_params=_cparams("arbitrary", "arbitrary"),
        name="ec_combine",
    )(starts, x, slot_t, mod, final_w, ye)


def _block_starts(rank, tb, cap):
    ng, ne = rank.shape[:2]
    first = rank[:, :, ::tb // LANES, 0]
    return jnp.concatenate([first, jnp.full((ng, ne, 1), cap, jnp.int32)], axis=2).reshape(-1)


def kernel(x_prompt, x_sample, c_prompt, c_sample, norm1_w, norm2_w, w_mod, b_mod, w_in, b_in, mlstm_norm_w, na_rpb, conv_w, conv_b, lru_wa, lru_ba, lru_wx, lru_bx, lru_L, w_br_a, w_br_b, w_br_c, w_out, w_router, b_router, w_gate_e, w_up_e, w_down_e, final_norm_w):
    bp, tp, d = x_prompt.shape
    bs, ts, _ = x_sample.shape
    lay = Layout(bp, tp, bs, ts, d)
    assert lay.n_p == lay.n_s, "the two request groups are routed as equal-sized token sets"
    depth = w_in.shape[0]
    ne = w_router.shape[2]
    ng, n = 2, lay.n_p
    cap = EC_FACTOR * n // ne
    tm_in = min(1024, ts)
    tn_in = 3 * d // 2
    tm_merge = min(512, ts)
    tt_lru = min(256, ts)
    tb_gather = min(1024, n)
    tb_comb = min(256, ts)
    rt = min(512, cap)

    x = jnp.concatenate([x_prompt.reshape(lay.n_p, d), x_sample.reshape(lay.n_s, d)], axis=0)
    nseq = bp + bs
    rows_c = -(-nseq // SUBLANES) * SUBLANES
    c_all = jnp.zeros((rows_c, d), F32).at[:nseq].set(jnp.concatenate([c_prompt, c_sample], axis=0))
    mod_all = _modulation(c_all, w_mod, b_mod).reshape(depth, rows_c, N_MOD, d)

    widths = (d, d, d, d, 4 * MLSTM_HEADS, d, d, d, d, d, N_BRANCH * d)
    offs = [0]
    for w in widths:
        offs.append(offs[-1] + w)
    col = lambda a, i: a[..., offs[i]:offs[i + 1]]
    bcols = (0, 1, 2, 5, 6, 7)
    fcols = (3, 8, 9, 10)

    for l in range(depth):
        wl, bl = w_in[l], b_in[l]
        wb = jnp.concatenate([col(wl, i) for i in bcols], axis=1).astype(BF16)
        bb = jnp.concatenate([col(bl, i) for i in bcols], axis=0).reshape(1, -1)
        wf = jnp.concatenate([col(wl, i) for i in fcols], axis=1).astype(BF16)
        bf = jnp.concatenate([col(bl, i) for i in fcols], axis=0).reshape(1, -1)
        ngate = 4 * MLSTM_HEADS
        wg = jnp.zeros((d, LANES), BF16).at[:, :ngate].set(col(wl, 4).astype(BF16))
        bg = jnp.zeros((1, LANES), F32).at[0, :ngate].set(col(bl, 4))
        mod = mod_all[l]

        proj_b, proj_f, gates = _input_projection(lay, x, mod, norm1_w[l].reshape(1, d), wb, bb, wf, bf, wg, bg, tm_in,
                                                   tn_in)
        gates_t = gates[:, :ngate].T

        h_fwd, h_rev = _mlstm(lay, proj_b, gates, gates_t)

        bias = _na_bias_table(na_rpb[l])
        hb_p = _neigh_attn(proj_b, bias, d, 0, bp, tp, 3)
        hb_s = _neigh_attn(proj_b, bias, d, lay.n_p, bs, ts, 3)

        lru_args = (conv_w[l], conv_b[l].reshape(1, d))
        l_fwd = _lru_pass(lay, proj_f, *lru_args, _lru_gate_weights(lru_wa[l, 0], lru_wx[l, 0]),
                          lru_ba[l, 0].reshape(1, d), lru_bx[l, 0].reshape(1, d), lru_L[l, 0].reshape(1, d),
                          None, rev=False, tt=tt_lru)
        h_c = _lru_pass(lay, proj_f, *lru_args, _lru_gate_weights(lru_wa[l, 1], lru_wx[l, 1]),
                        lru_ba[l, 1].reshape(1, d), lru_bx[l, 1].reshape(1, d), lru_L[l, 1].reshape(1, d),
                        l_fwd, rev=True, tt=tt_lru)

        x, xm2, logits_t = _merge_project(
            lay, h_fwd, h_rev, mlstm_norm_w[l].reshape(1, d), hb_p, hb_s, h_c, proj_f, x, mod,
            w_br_a[l].astype(BF16), w_br_b[l].astype(BF16),
            w_br_c[l].astype(BF16), w_out[l].astype(BF16), norm2_w[l].reshape(1, d),
            w_router[l].T.astype(BF16), b_router[l].reshape(ne, 1), tm_merge)

        slot, rank, aff = _route(logits_t, ng, cap)
        ye = _expert_ffn(_block_starts(rank, tb_gather, cap), xm2.reshape(ng, n, d), slot, aff,
                         w_gate_e[l].astype(BF16), w_up_e[l].astype(BF16), w_down_e[l].astype(BF16),
                         cap, tb_gather, rt)
        slot_t = slot.reshape(ng, ne, n).transpose(0, 2, 1)
        outs = _combine(lay, _block_starts(rank, tb_comb, cap), x, slot_t, mod, final_norm_w.reshape(1, d), ye,
                        tb_comb, final=(l == depth - 1))
        x = outs[0]

    return (outs[0].reshape(bp, tp, d), outs[1].reshape(bs, ts, d))
```

```python
import functools
from typing import NamedTuple

import jax
import jax.numpy as jnp
from jax import lax
from jax.experimental import pallas as pl
from jax.experimental.pallas import tpu as pltpu

EPS = 1e-6
NEG_INF = -1e30
GRID_W = 64
MLSTM_HEADS = 4
MLSTM_CHUNK = 128
NA_HEADS = 16
NA_KH = 8
NA_KW = 16
LRU_BLOCKS = 16
LRU_C = 8.0
CONV_W = 4
N_BRANCH = 3
N_EXPERTS = 16
EC_FACTOR = 2
N_MOD = 6

LANES = 128
SUBLANES = 8
BF16_ROWS = 16
MXU_DIM = 256
VMEM_LIMIT = 56 * 1024 * 1024

F32 = jnp.float32
BF16 = jnp.bfloat16


class Layout(NamedTuple):
    bp: int
    tp: int
    bs: int
    ts: int
    d: int

    @property
    def n_p(self):
        return self.bp * self.tp

    @property
    def n_s(self):
        return self.bs * self.ts

    @property
    def nt(self):
        return self.n_p + self.n_s

    def seq_index(self, tok0):
        return jnp.where(tok0 < self.n_p, tok0 // self.tp, self.bp + (tok0 - self.n_p) // self.ts)

    def seq_len(self, tok0):
        return jnp.where(tok0 < self.n_p, self.tp, self.ts)

    def pos_in_seq(self, tok0):
        return jnp.where(tok0 < self.n_p, tok0 % self.tp, (tok0 - self.n_p) % self.ts)


def _cparams(*sem):
    return pltpu.CompilerParams(dimension_semantics=sem, vmem_limit_bytes=VMEM_LIMIT)


def _dot(a, b):
    return jnp.dot(a, b, preferred_element_type=F32)


def _dot_nt(a, b):
    return lax.dot_general(a, b, (((1,), (1,)), ((), ())), preferred_element_type=F32)


def _dot_tn(a, b):
    return lax.dot_general(a, b, (((0,), (0,)), ((), ())), preferred_element_type=F32)


def _sigmoid(x):
    return 1.0 / (1.0 + jnp.exp(-x))


def _rms(x, w):
    return x * lax.rsqrt(jnp.mean(x * x, axis=-1, keepdims=True) + EPS) * w


def _mod_body(c_ref, w_ref, b_ref, o_ref):
    c = c_ref[...]
    ca = (c * _sigmoid(c)).astype(BF16)
    o_ref[0] = _dot(ca, w_ref[0].astype(BF16)) + b_ref[0]


def _modulation(c_all, w_mod, b_mod):
    depth, d, nm = w_mod.shape
    rows = c_all.shape[0]
    tn = 1536
    return pl.pallas_call(
        _mod_body,
        grid=(depth, nm // tn),
        in_specs=[pl.BlockSpec((rows, d), lambda l, j: (0, 0)),
                  pl.BlockSpec((1, d, tn), lambda l, j: (l, 0, j)),
                  pl.BlockSpec((1, 1, tn), lambda l, j: (l, 0, j))],
        out_specs=pl.BlockSpec((1, rows, tn), lambda l, j: (l, 0, j)),
        out_shape=jax.ShapeDtypeStruct((depth, rows, nm), F32),
        compiler_params=_cparams("arbitrary", "arbitrary"),
        name="adaln_modulation",
    )(c_all, w_mod, b_mod.reshape(depth, 1, nm))


def _inproj_body(x_ref, mod_ref, nw_ref, wb_ref, bb_ref, wf_ref, bf_ref, wg_ref, bg_ref,
                 ob_ref, of_ref, og_ref, xm_ref, *, n_bf):
    j = pl.program_id(1)

    @pl.when(j == 0)
    def _():
        m = mod_ref[0]
        xm = (_rms(x_ref[...], nw_ref[...]) * (1.0 + m[1:2]) + m[0:1]).astype(BF16)
        xm_ref[...] = xm
        og_ref[...] = _dot(xm, wg_ref[...]) + bg_ref[...]

    @pl.when(j < n_bf)
    def _():
        ob_ref[...] = (_dot(xm_ref[...], wb_ref[...]) + bb_ref[...]).astype(BF16)

    @pl.when(j >= n_bf)
    def _():
        of_ref[...] = (_dot(xm_ref[...], wf_ref[...]) + bf_ref[...]).astype(of_ref.dtype)


def _input_projection(lay, x, mod, norm_w, wb, bb, wf, bf, wg, bg, tm, tn):
    nt, d = x.shape
    n_bf, n_f = wb.shape[1] // tn, wf.shape[1] // tn
    seq = lambda i: lay.seq_index(i * tm)
    jb = lambda j: jnp.minimum(j, n_bf - 1)
    jf = lambda j: jnp.maximum(j - n_bf, 0)
    return pl.pallas_call(
        functools.partial(_inproj_body, n_bf=n_bf),
        grid=(nt // tm, n_bf + n_f),
        in_specs=[pl.BlockSpec((tm, d), lambda i, j: (i, 0)),
                  pl.BlockSpec((1, N_MOD, d), lambda i, j: (seq(i), 0, 0)),
                  pl.BlockSpec((1, d), lambda i, j: (0, 0)),
                  pl.BlockSpec((d, tn), lambda i, j: (0, jb(j))),
                  pl.BlockSpec((1, tn), lambda i, j: (0, jb(j))),
                  pl.BlockSpec((d, tn), lambda i, j: (0, jf(j))),
                  pl.BlockSpec((1, tn), lambda i, j: (0, jf(j))),
                  pl.BlockSpec((d, LANES), lambda i, j: (0, 0)),
                  pl.BlockSpec((1, LANES), lambda i, j: (0, 0))],
        out_specs=[pl.BlockSpec((tm, tn), lambda i, j: (i, jb(j))),
                   pl.BlockSpec((tm, tn), lambda i, j: (i, jf(j))),
                   pl.BlockSpec((tm, LANES), lambda i, j: (i, 0))],
        out_shape=[jax.ShapeDtypeStruct((nt, wb.shape[1]), BF16),
                   jax.ShapeDtypeStruct((nt, wf.shape[1]), BF16),
                   jax.ShapeDtypeStruct((nt, LANES), F32)],
        scratch_shapes=[pltpu.VMEM((tm, d), BF16)],
        compiler_params=_cparams("arbitrary", "arbitrary"),
        name="norm_inproj",
    )(x, mod, norm_w, wb, bb, wf, bf, wg, bg)


def _lane_cumsum(x, rev):
    n = x.shape[-1]
    lane = lax.broadcasted_iota(jnp.int32, x.shape, 1)
    k = 1
    while k < n:
        if rev:
            x = x + jnp.where(lane < n - k, pltpu.roll(x, n - k, 1), 0.0)
        else:
            x = x + jnp.where(lane >= k, pltpu.roll(x, k, 1), 0.0)
        k *= 2
    return x


def _log_sigmoid(x):
    return jnp.minimum(x, 0.0) - jnp.log(1.0 + jnp.exp(-jnp.abs(x)))


def _mlstm_body(qf_ref, kf_ref, vf_ref, grf_ref, gcf_ref, qb_ref, kb_ref, vb_ref, grb_ref, gcb_ref,
                hf_ref, hb_ref, c_ref, m_ref, wi_ref, col_ref, sc_ref, qc_ref, *, lay, heads, chunk):
    s = pl.program_id(0)
    tok_f = s * chunk
    tok_b = (pl.num_programs(0) - 1 - s) * chunk
    dh = qf_ref.shape[1] // heads

    def reset(lo):
        c_ref[lo:lo + heads] = jnp.zeros((heads,) + c_ref.shape[1:], F32)
        m_ref[lo:lo + heads] = jnp.zeros((heads,) + m_ref.shape[1:], F32)

    @pl.when(lay.pos_in_seq(tok_f) == 0)
    def _():
        reset(0)

    @pl.when(lay.pos_in_seq(tok_b) + chunk == lay.seq_len(tok_b))
    def _():
        reset(heads)

    sides = ((qf_ref, kf_ref, vf_ref, grf_ref, gcf_ref, hf_ref), (qb_ref, kb_ref, vb_ref, grb_ref, gcb_ref, hb_ref))
    units = [(rev, h) for rev in (0, 1) for h in range(heads)]
    ri = lax.broadcasted_iota(jnp.int32, (chunk, chunk), 0)
    cj = lax.broadcasted_iota(jnp.int32, (chunk, chunk), 1)
    kscale = dh ** -0.5
    ones = jnp.ones((chunk, LANES), BF16)
    rep = lambda col: jnp.broadcast_to(col, (chunk, LANES))

    def scaled_k(rev, h):
        return (sides[rev][1][:, h * dh:(h + 1) * dh].astype(F32) * kscale).astype(BF16)

    decays, m_news = [], []
    for rev in (0, 1):
        gr = sides[rev][3][...]
        gc = sides[rev][4][...]
        gi = 2 * heads * rev
        lf_all = _log_sigmoid(gr)
        b_all = _lane_cumsum(lf_all, bool(rev))
        mask = (cj >= ri) if rev else (cj <= ri)
        for h in range(heads):
            u = rev * heads + h
            ig_row = gr[gi + h:gi + h + 1, :]
            lf_row = lf_all[gi + heads + h:gi + heads + h + 1, :]
            b_row = b_all[gi + heads + h:gi + heads + h + 1, :]
            ig_col = gc[:, gi + h:gi + h + 1]
            m_old = m_ref[u][:, 0:1]
            b_col = jnp.sum(jnp.where(mask, lf_row, 0.0), axis=1, keepdims=True)
            dmat = jnp.where(mask, b_col - b_row + ig_row, NEG_INF)
            g = b_col + m_old
            m_row = jnp.maximum(g, jnp.max(dmat, axis=1, keepdims=True))
            wi_ref[u] = jnp.exp(dmat - m_row)
            b_last = b_row[:, 0:1] if rev else b_row[:, chunk - 1:chunk]
            m_new = jnp.maximum(b_last + m_old, jnp.max(b_last - b_row + ig_row, axis=1, keepdims=True))
            col_ref[u, 0] = rep(jnp.exp(g - m_row))
            col_ref[u, 1] = rep(jnp.exp(-m_row))
            col_ref[u, 2] = rep(jnp.exp(b_last - b_col + ig_col - m_new))
            decays.append(jnp.exp(b_last + m_old - m_new))
            m_news.append(m_new)

    for rev, h in units:
        u = rev * heads + h
        q = sides[rev][0][:, h * dh:(h + 1) * dh]
        sc_ref[u] = (_dot_nt(q, scaled_k(rev, h)) * wi_ref[u]).astype(BF16)
    for rev, h in units:
        u = rev * heads + h
        q = sides[rev][0][:, h * dh:(h + 1) * dh]
        qc_ref[u] = _dot(q, c_ref[u].astype(BF16))

    for rev, h in units:
        u = rev * heads + h
        sl = slice(h * dh, (h + 1) * dh)
        v_ext = jnp.concatenate([sides[rev][2][:, sl], ones], axis=1)
        w_inter = col_ref[u, 0]
        ne = _dot(sc_ref[u], v_ext) + jnp.concatenate([w_inter] * (dh // LANES + 1), axis=1) * qc_ref[u]
        den = jnp.maximum(jnp.abs(ne[:, dh:]), col_ref[u, 1])
        sides[rev][5][:, sl] = ne[:, :dh] / jnp.concatenate([den] * (dh // LANES), axis=1)

    for rev, h in units:
        u = rev * heads + h
        sl = slice(h * dh, (h + 1) * dh)
        v_ext = jnp.concatenate([sides[rev][2][:, sl], ones], axis=1)
        kw = scaled_k(rev, h).astype(F32) * jnp.concatenate([col_ref[u, 2]] * (dh // LANES), axis=1)
        c_ref[u] = decays[u] * c_ref[u] + _dot_tn(kw.astype(BF16), v_ext)
        m_ref[u] = jnp.broadcast_to(m_news[u], m_ref.shape[1:])


def _mlstm(lay, proj_b, gates, gates_t):
    nt, d = lay.nt, lay.d
    chunk, heads = MLSTM_CHUNK, MLSTM_HEADS
    nc = nt // chunk
    dh = d // heads
    nu = 2 * heads

    def side(cidx):
        return [pl.BlockSpec((chunk, d), lambda s: (cidx(s), 0)),
                pl.BlockSpec((chunk, d), lambda s: (cidx(s), 1)),
                pl.BlockSpec((chunk, d), lambda s: (cidx(s), 2)),
                pl.BlockSpec((4 * heads, chunk), lambda s: (0, cidx(s))),
                pl.BlockSpec((chunk, LANES), lambda s: (cidx(s), 0))]

    fwd = lambda s: s
    bwd = lambda s: nc - 1 - s
    args = [proj_b, proj_b, proj_b, gates_t, gates]
    return pl.pallas_call(
        functools.partial(_mlstm_body, lay=lay, heads=heads, chunk=chunk),
        grid=(nc,),
        in_specs=side(fwd) + side(bwd),
        out_specs=[pl.BlockSpec((chunk, d), lambda s: (fwd(s), 0)), pl.BlockSpec((chunk, d), lambda s: (bwd(s), 0))],
        out_shape=[jax.ShapeDtypeStruct((nt, d), F32), jax.ShapeDtypeStruct((nt, d), F32)],
        scratch_shapes=[pltpu.VMEM((nu, dh, dh + LANES), F32), pltpu.VMEM((nu, 1, LANES), F32),
                        pltpu.VMEM((nu, chunk, chunk), F32), pltpu.VMEM((nu, 3, chunk, LANES), F32),
                        pltpu.VMEM((nu, chunk, chunk), BF16), pltpu.VMEM((nu, chunk, dh + LANES), F32)],
        compiler_params=_cparams("arbitrary"),
        name="mlstm_bidir",
    )(*(args + args))


def _na_bias_table(rpb):
    h = rpb.shape[0]
    cols = jnp.arange(GRID_W)
    cs = jnp.clip(cols - NA_KW // 2, 0, GRID_W - NA_KW)
    col_in = (cols[None, :] >= cs[:, None]) & (cols[None, :] < cs[:, None] + NA_KW)
    dc_idx = jnp.clip(cols[None, :] - cols[:, None], -(NA_KW - 1), NA_KW - 1) + NA_KW - 1
    t = jnp.where(col_in[None, None], rpb[:, :, dc_idx], NEG_INF)
    d_idx = jnp.arange(NA_KH)[:, None] + jnp.arange(NA_KH)[None, :]
    tc = t[:, d_idx]
    tc = tc.transpose(0, 1, 3, 2, 4).reshape(h, NA_KH, GRID_W, NA_KH * GRID_W)
    return tc.reshape(h // 2, 2, NA_KH, GRID_W, NA_KH * GRID_W).transpose(0, 2, 1, 3, 4).reshape(
        h // 2, NA_KH, 2 * GRID_W, NA_KH * GRID_W).astype(F32)


def _na_body(q_ref, k_ref, v_ref, bias_ref, o_ref, s_ref, p_ref, l_ref, *, rows, rblk, dh):
    rb = pl.program_id(2)
    lane_q = lax.broadcasted_iota(jnp.int32, (GRID_W, 2 * dh), 1)
    qscale = dh ** -0.5

    starts, scores = [], []
    for r in range(rblk):
        rg = rb * rblk + r
        rs = jnp.clip(rg - NA_KH // 2, 0, rows - NA_KH)
        d0 = rs - rg + NA_KH - 1
        q = q_ref[r * GRID_W:(r + 1) * GRID_W, :].astype(F32) * qscale
        q2 = jnp.concatenate([jnp.where(lane_q < dh, q, 0.0), jnp.where(lane_q >= dh, q, 0.0)], axis=0).astype(BF16)
        k0 = pl.multiple_of(rs * GRID_W, GRID_W)
        starts.append(k0)
        s_ref[r] = _dot_nt(q2, k_ref[pl.ds(k0, NA_KH * GRID_W), :]) + bias_ref[0, d0]
    for r in range(rblk):
        s = s_ref[r]
        p = jnp.exp(s - jnp.max(s, axis=1, keepdims=True))
        l_ref[r] = jnp.broadcast_to(jnp.sum(p, axis=1, keepdims=True), l_ref.shape[1:])
        p_ref[r] = p.astype(BF16)
    for r in range(rblk):
        o = _dot(p_ref[r], v_ref[pl.ds(starts[r], NA_KH * GRID_W), :]) / l_ref[r]
        out = jnp.where(lane_q < dh, o[:GRID_W], o[GRID_W:])
        o_ref[r * GRID_W:(r + 1) * GRID_W, :] = out.astype(o_ref.dtype)


def _neigh_attn(proj_b, bias, d, tok_off, nseq, t, col0):
    rows = t // GRID_W
    rblk = min(16, rows)
    dh = d // NA_HEADS
    pairs = NA_HEADS // 2
    pw = 2 * dh
    per = d // pw
    seq0 = tok_off // t
    qblk = rblk * GRID_W
    q0 = tok_off // qblk
    nrb = rows // rblk
    return pl.pallas_call(
        functools.partial(_na_body, rows=rows, rblk=rblk, dh=dh),
        grid=(pairs, nseq, nrb),
        in_specs=[pl.BlockSpec((qblk, pw), lambda p, b, r: (q0 + b * nrb + r, col0 * per + p)),
                  pl.BlockSpec((t, pw), lambda p, b, r: (seq0 + b, (col0 + 1) * per + p)),
                  pl.BlockSpec((t, pw), lambda p, b, r: (seq0 + b, (col0 + 2) * per + p)),
                  pl.BlockSpec((1, NA_KH, 2 * GRID_W, NA_KH * GRID_W), lambda p, b, r: (p, 0, 0, 0))],
        out_specs=pl.BlockSpec((qblk, pw), lambda p, b, r: (b * nrb + r, p)),
        out_shape=jax.ShapeDtypeStruct((nseq * t, d), BF16),
        scratch_shapes=[pltpu.VMEM((rblk, 2 * GRID_W, NA_KH * GRID_W), F32),
                        pltpu.VMEM((rblk, 2 * GRID_W, NA_KH * GRID_W), BF16),
                        pltpu.VMEM((rblk, 2 * GRID_W, LANES), F32)],
        compiler_params=_cparams("arbitrary", "arbitrary", "arbitrary"),
        name="neigh_attn",
    )(proj_b, proj_b, proj_b, bias)


def _softplus(x):
    return jnp.maximum(x, 0.0) + jnp.log(1.0 + jnp.exp(-jnp.abs(x)))


def _gelu_tanh(x):
    return 0.5 * x * (1.0 + jnp.tanh(0.7978845608028654 * (x + 0.044715 * (x * x * x))))


def _lru_body(*refs, lay, rev, tt):
    if rev:
        (x_ref, xp_ref, xn_ref, cw_ref, cb_ref, w_ref, ba_ref, bx_ref, lam_ref, hf_ref, y_ref,
         out_ref, carry_ref) = refs
    else:
        x_ref, xp_ref, xn_ref, cw_ref, cb_ref, w_ref, ba_ref, bx_ref, lam_ref, out_ref, carry_ref = refs
    s = pl.program_id(0)
    ti = (pl.num_programs(0) - 1 - s) if rev else s
    tok0 = ti * tt
    pos = lay.pos_in_seq(tok0)
    slen = lay.seq_len(tok0)
    is_first = pos == 0
    is_last = pos + tt == slen

    @pl.when(is_last if rev else is_first)
    def _():
        carry_ref[...] = jnp.zeros_like(carry_ref)

    d = x_ref.shape[1]
    x = x_ref[...].astype(F32)
    halo = SUBLANES
    xprev = jnp.where(is_first, 0.0, xp_ref[BF16_ROWS - halo:, :].astype(F32))
    xnext = jnp.where(is_last, 0.0, xn_ref[:halo, :].astype(F32))
    xe = jnp.concatenate([xprev, x, xnext], axis=0)
    pad = CONV_W // 2
    xc = cb_ref[...]
    for j in range(CONV_W):
        o = halo - pad + j
        xc = xc + xe[o:o + tt] * cw_ref[j:j + 1, :]
    sp = _softplus(-lam_ref[...])
    ng = tt // SUBLANES
    sub = lax.broadcasted_iota(jnp.int32, (ng, SUBLANES, MXU_DIM), 1)
    for t in range(d // MXU_DIM):
        cs = slice(t * MXU_DIM, (t + 1) * MXU_DIM)
        xct = xc[:, cs]
        gg = _dot(xct.astype(BF16), w_ref[t])
        r = _sigmoid(gg[:, :MXU_DIM] + ba_ref[:, cs])
        i = _sigmoid(gg[:, MXU_DIM:] + bx_ref[:, cs])
        log_a = -LRU_C * r * sp[:, cs]
        a = jnp.exp(log_a)
        bt = jnp.sqrt(1.0 - a * a) * (i * xct)
        a3 = a.reshape(ng, SUBLANES, MXU_DIM)
        b3 = bt.reshape(ng, SUBLANES, MXU_DIM)
        k = 1
        while k < SUBLANES:
            if rev:
                ok = sub < SUBLANES - k
                ash = pltpu.roll(a3, SUBLANES - k, 1)
                bsh = pltpu.roll(b3, SUBLANES - k, 1)
            else:
                ok = sub >= k
                ash = pltpu.roll(a3, k, 1)
                bsh = pltpu.roll(b3, k, 1)
            b3 = jnp.where(ok, a3 * bsh + b3, b3)
            a3 = jnp.where(ok, a3 * ash, a3)
            k *= 2
        hprev = carry_ref[:, cs]
        outs = [None] * ng
        order = range(ng - 1, -1, -1) if rev else range(ng)
        for gidx in order:
            hg = a3[gidx] * hprev + b3[gidx]
            outs[gidx] = hg
            hprev = hg[0:1] if rev else hg[SUBLANES - 1:SUBLANES]
        carry_ref[:, cs] = hprev
        hh = jnp.concatenate(outs, axis=0)
        if rev:
            out_ref[:, cs] = ((hf_ref[:, cs] + hh) * _gelu_tanh(y_ref[:, cs].astype(F32))).astype(out_ref.dtype)
        else:
            out_ref[:, cs] = hh


def _lru_pass(lay, proj_f, conv_w, conv_b, w_dir, ba, bx, lam, h_fwd, rev, tt):
    nt, d = lay.nt, lay.d
    ntile = nt // tt
    hb = tt // BF16_ROWS
    nhb = nt // BF16_ROWS
    tidx = (lambda s: ntile - 1 - s) if rev else (lambda s: s)
    xcol = 1
    in_specs = [pl.BlockSpec((tt, d), lambda s: (tidx(s), xcol)),
                pl.BlockSpec((BF16_ROWS, d), lambda s: (jnp.maximum(tidx(s) * hb - 1, 0), xcol)),
                pl.BlockSpec((BF16_ROWS, d), lambda s: (jnp.minimum((tidx(s) + 1) * hb, nhb - 1), xcol)),
                pl.BlockSpec((CONV_W, d), lambda s: (0, 0)),
                pl.BlockSpec((1, d), lambda s: (0, 0)),
                pl.BlockSpec((d // MXU_DIM, MXU_DIM, 2 * MXU_DIM), lambda s: (0, 0, 0)),
                pl.BlockSpec((1, d), lambda s: (0, 0)),
                pl.BlockSpec((1, d), lambda s: (0, 0)),
                pl.BlockSpec((1, d), lambda s: (0, 0))]
    args = [proj_f, proj_f, proj_f, conv_w, conv_b, w_dir, ba, bx, lam]
    if rev:
        in_specs += [pl.BlockSpec((tt, d), lambda s: (tidx(s), 0)),
                     pl.BlockSpec((tt, d), lambda s: (tidx(s), 2))]
        args += [h_fwd, proj_f]
    return pl.pallas_call(
        functools.partial(_lru_body, lay=lay, rev=rev, tt=tt),
        grid=(ntile,),
        in_specs=in_specs,
        out_specs=pl.BlockSpec((tt, d), lambda s: (tidx(s), 0)),
        out_shape=jax.ShapeDtypeStruct((nt, d), BF16 if rev else F32),
        scratch_shapes=[pltpu.VMEM((1, d), F32)],
        compiler_params=_cparams("arbitrary"),
        name="rglru_bwd" if rev else "rglru_fwd",
    )(*args)


def _lru_gate_weights(wa, wx):
    nb, bw, _ = wa.shape
    per = MXU_DIM // bw
    nt = nb // per

    def diag(w):
        w = w.reshape(nt, per, bw, bw)
        eye = jnp.eye(per, dtype=w.dtype)
        return jnp.einsum('tpkj,pq->tpkqj', w, eye).reshape(nt, MXU_DIM, MXU_DIM)

    return jnp.concatenate([diag(wa), diag(wx)], axis=-1).astype(BF16)


def _merge_body(hf_ref, hr_ref, o_ref, hnw_ref, hbp_ref, hbs_ref, hc_ref, g0_ref, g1_ref, g2_ref, x_ref, mod_ref,
                wa_ref, wb_ref, wc_ref, wo_ref, nw_ref, wr_ref, br_ref, xo_ref, xm_ref, lg_ref, *, heads, np_tiles):
    m = mod_ref[0]
    h_b = jnp.where(pl.program_id(0) < np_tiles, hbp_ref[...], hbs_ref[...])
    dh = hf_ref.shape[1] // heads
    parts = []
    for h in range(heads):
        sl = slice(h * dh, (h + 1) * dh)
        hs = hf_ref[:, sl] + hr_ref[:, sl]
        mu = jnp.mean(hs, axis=1, keepdims=True)
        var = jnp.mean(jnp.square(hs - mu), axis=1, keepdims=True)
        hn = (hs - mu) * lax.rsqrt(var + EPS) * hnw_ref[:, sl]
        parts.append((_sigmoid(o_ref[:, sl].astype(F32)) * hn).astype(BF16))
    h_a = jnp.concatenate(parts, axis=1)
    gate = lambda ref: _sigmoid(ref[...].astype(F32))
    merged = (gate(g0_ref) * _dot(h_a, wa_ref[...]) + gate(g1_ref) * _dot(h_b, wb_ref[...])
              + gate(g2_ref) * _dot(hc_ref[...], wc_ref[...]))
    xn = x_ref[...] + m[2:3] * _dot(merged.astype(BF16), wo_ref[...])
    xo_ref[...] = xn
    xm = (_rms(xn, nw_ref[...]) * (1.0 + m[4:5]) + m[3:4]).astype(BF16)
    xm_ref[...] = xm
    lg_ref[...] = _dot_nt(wr_ref[...], xm) + br_ref[...]


def _merge_project(lay, h_fwd, h_rev, mlstm_nw, hb_p, hb_s, h_c, proj_f, x, mod, wa, wb, wc, wo, norm_w, wr_t, br_t,
                   tm):
    nt, d = x.shape
    ne = wr_t.shape[0]
    seq = lambda i: lay.seq_index(i * tm)
    act = lambda c: pl.BlockSpec((tm, d), lambda i: (i, c))
    const = lambda shape: pl.BlockSpec(shape, lambda i: (0,) * len(shape), pipeline_mode=pl.Buffered(1))
    np_tiles = lay.n_p // tm
    return pl.pallas_call(
        functools.partial(_merge_body, heads=MLSTM_HEADS, np_tiles=np_tiles),
        grid=(nt // tm,),
        in_specs=[act(0), act(0), act(0), const((1, d)),
                  pl.BlockSpec((tm, d), lambda i: (jnp.minimum(i, np_tiles - 1), 0)),
                  pl.BlockSpec((tm, d), lambda i: (jnp.maximum(i - np_tiles, 0), 0)),
                  act(0), act(3), act(4), act(5), act(0),
                  pl.BlockSpec((1, N_MOD, d), lambda i: (seq(i), 0, 0)),
                  const((d, d)), const((d, d)), const((d, d)), const((d, d)),
                  const((1, d)), const((ne, d)), const((ne, 1))],
        out_specs=[pl.BlockSpec((tm, d), lambda i: (i, 0)),
                   pl.BlockSpec((tm, d), lambda i: (i, 0)),
                   pl.BlockSpec((ne, tm), lambda i: (0, i))],
        out_shape=[jax.ShapeDtypeStruct((nt, d), F32),
                   jax.ShapeDtypeStruct((nt, d), BF16),
                   jax.ShapeDtypeStruct((ne, nt), F32)],
        compiler_params=_cparams("arbitrary"),
        name="merge_outproj",
    )(h_fwd, h_rev, proj_f, mlstm_nw, hb_p, hb_s, h_c, proj_f, proj_f, proj_f, x, mod, wa, wb, wc, wo, norm_w, wr_t, br_t)


def _prefix_count(flags, tri_u, tri_l):
    ne, nr, nl = flags.shape
    fb = flags.astype(BF16)
    inc = _dot(fb.reshape(ne * nr, nl), tri_u).reshape(ne, nr, nl)
    tot = jnp.broadcast_to(inc[:, :, nl - 1:nl], (ne, nr, nl)).astype(BF16)
    offs = jnp.stack([_dot(tri_l, tot[e]) for e in range(ne)], axis=0)
    return offs + inc - flags


def _route_body(lg_ref, slot_ref, rank_ref, aff_ref, *, cap):
    lg = lg_ref[0]
    ne, nr, nl = lg.shape
    ex = jnp.exp(lg - jnp.max(lg, axis=0, keepdims=True))
    aff = ex / jnp.sum(ex, axis=0, keepdims=True)
    bits = pltpu.bitcast(aff, jnp.int32)

    def count(flags):
        return jnp.sum(jnp.sum(flags, axis=1, keepdims=True), axis=2, keepdims=True)

    def bisect(i, thr):
        cand = thr | lax.shift_left(jnp.int32(1), 30 - i)
        cnt = count(jnp.where(bits >= cand, 1.0, 0.0))
        return jnp.where(cnt >= cap, cand, thr)

    thr = lax.fori_loop(0, 31, bisect, jnp.zeros((ne, 1, 1), jnp.int32))
    li = lax.broadcasted_iota(jnp.int32, (nl, nl), 0)
    lj = lax.broadcasted_iota(jnp.int32, (nl, nl), 1)
    tri_u = jnp.where(li <= lj, 1.0, 0.0).astype(BF16)
    ri = lax.broadcasted_iota(jnp.int32, (nr, nr), 0)
    rj = lax.broadcasted_iota(jnp.int32, (nr, nr), 1)
    tri_l = jnp.where(rj < ri, 1.0, 0.0).astype(BF16)
    gt = jnp.where(bits > thr, 1.0, 0.0)
    eq = jnp.where(bits == thr, 1.0, 0.0)
    need = cap - count(gt)
    sel = gt + eq * jnp.where(_prefix_count(eq, tri_u, tri_l) < need, 1.0, 0.0)
    rank = _prefix_count(sel, tri_u, tri_l).astype(jnp.int32)
    rank_ref[0] = rank
    slot_ref[0] = jnp.where(sel > 0.5, rank, -1)
    aff_ref[0] = aff


def _route(logits_t, ng, cap):
    ne, nall = logits_t.shape
    n = nall // ng
    nr = n // LANES
    lg = logits_t.reshape(ne, ng, nr, LANES).transpose(1, 0, 2, 3)
    spec = pl.BlockSpec((1, ne, nr, LANES), lambda g: (g, 0, 0, 0))
    return pl.pallas_call(
        functools.partial(_route_body, cap=cap),
        grid=(ng,),
        in_specs=[spec],
        out_specs=[spec, spec, spec],
        out_shape=[jax.ShapeDtypeStruct((ng, ne, nr, LANES), jnp.int32),
                   jax.ShapeDtypeStruct((ng, ne, nr, LANES), jnp.int32),
                   jax.ShapeDtypeStruct((ng, ne, nr, LANES), F32)],
        compiler_params=_cparams("arbitrary"),
        name="ec_route",
    )(lg)


def _window_count(s0, s1):
    return jnp.where(s1 > s0, (s1 - 1) // LANES - s0 // LANES + 1, 0)


def _expert_body(st_ref, x_ref, slot_ref, aff_ref, wg_ref, wu_ref, wd_ref, ye_ref, xe_ref, gate_ref,
                 *, nb, nsub, span, cap, rt, fc):
    g, e, s = pl.program_id(0), pl.program_id(1), pl.program_id(2)
    ne = pl.num_programs(1)

    @pl.when(s == 0)
    def _():
        xe_ref[...] = jnp.zeros_like(xe_ref)
        gate_ref[...] = jnp.zeros_like(gate_ref)

    @pl.when(s < nb)
    def _():
        tsub = x_ref.shape[1] // nsub
        rsub = tsub // LANES
        sub = lax.broadcasted_iota(jnp.int32, (span, LANES), 0)

        def gather(row0, want, slot, aff, xs):
            tgt = sub + row0
            hits = [slot[c:c + 1, :] == tgt for c in range(rsub)]
            if want is not None:
                hits = [h & (slot[c:c + 1, :] >= want) for c, h in enumerate(hits)]
            onehot = jnp.concatenate([jnp.where(h, 1.0, 0.0).astype(BF16) for h in hits], axis=1)
            xe_ref[pl.ds(row0, span), :] += _dot(onehot, xs).astype(BF16)
            gsum = functools.reduce(lambda a, b: a + b,
                                    [jnp.where(h, aff[c:c + 1, :], 0.0) for c, h in enumerate(hits)])
            gate_ref[pl.ds(row0, span), :] += jnp.sum(gsum, axis=1, keepdims=True)

        for j in range(nsub):
            base = (g * ne + e) * (nb * nsub + 1) + s * nsub + j
            s0, s1 = st_ref[base], st_ref[base + 1]
            slot = slot_ref[0, 0, j * rsub:(j + 1) * rsub, :]
            aff = aff_ref[0, 0, j * rsub:(j + 1) * rsub, :]
            xs = x_ref[0, j * tsub:(j + 1) * tsub, :]
            first = pl.multiple_of(jnp.minimum((s0 // BF16_ROWS) * BF16_ROWS, cap - span), BF16_ROWS)
            gather(first, None, slot, aff, xs)

            def extra(w, carry):
                want = first + span + w * span
                row0 = pl.multiple_of(jnp.minimum(want, cap - span), BF16_ROWS)
                gather(row0, want, slot, aff, xs)
                return carry

            lax.fori_loop(0, (jnp.maximum(s1 - first - span, 0) + span - 1) // span, extra, 0)

    @pl.when(s >= nb)
    def _():
        r = s - nb
        rows = pl.ds(pl.multiple_of(r * rt, rt), rt)
        xe = xe_ref[rows, :]
        f = wg_ref.shape[2]
        acc = None
        for c in range(f // fc):
            fs = slice(c * fc, (c + 1) * fc)
            gg = _dot(xe, wg_ref[0, :, fs])
            hid = (gg * _sigmoid(gg) * _dot(xe, wu_ref[0, :, fs])).astype(BF16)
            part = _dot(hid, wd_ref[0, fs, :])
            acc = part if acc is None else acc + part
        ye_ref[0, 0] = (acc * gate_ref[rows, :]).astype(ye_ref.dtype)


def _expert_ffn(starts, xm, slot, aff, wg, wu, wd, cap, tb, tsub, rt):
    ng, n, d = xm.shape
    ne, _, f = wg.shape
    nb = n // tb
    nr = cap // rt
    span = min(LANES, cap)
    tok = lambda g, e, s, st: (g, e, jnp.minimum(s, nb - 1), 0)
    grid_spec = pltpu.PrefetchScalarGridSpec(
        num_scalar_prefetch=1,
        grid=(ng, ne, nb + nr),
        in_specs=[pl.BlockSpec((1, tb, d), lambda g, e, s, st: (g, jnp.minimum(s, nb - 1), 0)),
                  pl.BlockSpec((1, 1, tb // LANES, LANES), tok),
                  pl.BlockSpec((1, 1, tb // LANES, LANES), tok),
                  pl.BlockSpec((1, d, f), lambda g, e, s, st: (e, 0, 0)),
                  pl.BlockSpec((1, d, f), lambda g, e, s, st: (e, 0, 0)),
                  pl.BlockSpec((1, f, d), lambda g, e, s, st: (e, 0, 0))],
        out_specs=pl.BlockSpec((1, 1, rt, d), lambda g, e, s, st: (g, e, jnp.maximum(s - nb, 0), 0)),
        scratch_shapes=[pltpu.VMEM((cap, d), BF16), pltpu.VMEM((cap, 1), F32)])
    return pl.pallas_call(
        functools.partial(_expert_body, nb=nb, nsub=tb // tsub, span=span, cap=cap, rt=rt, fc=min(512, f)),
        grid_spec=grid_spec,
        out_shape=jax.ShapeDtypeStruct((ng, ne, cap, d), BF16),
        compiler_params=_cparams("arbitrary", "arbitrary", "arbitrary"),
        name="ec_expert_ffn",
    )(starts, xm, slot, aff, wg, wu, wd)


def _combine_body(st_ref, x_ref, slot_ref, mod_ref, fw_ref, ye_hbm, *rest, nb, tb, ne, span, cap, final):
    o_refs = rest[:-5]
    buf_ref, sem_ref, xbuf_ref, xsem_ref, acc_ref = rest[-5:]
    g, b = pl.program_id(0), pl.program_id(1)
    step = g * nb + b
    nsteps = pl.num_programs(0) * nb
    par = step % 2

    def bounds(gg, bb, e):
        base = (gg * ne + e) * (nb + 1) + bb
        return st_ref[base], st_ref[base + 1]

    def span_start(gg, bb, e):
        s0, _ = bounds(gg, bb, e)
        return pl.multiple_of(jnp.minimum((s0 // BF16_ROWS) * BF16_ROWS, cap - span), BF16_ROWS)

    def span_copies(gg, bb, parity):
        return [pltpu.make_async_copy(ye_hbm.at[gg, e, pl.ds(span_start(gg, bb, e), span), :],
                                      buf_ref.at[parity, e], sem_ref.at[parity, e]) for e in range(ne)]

    @pl.when(step == 0)
    def _():
        for cp in span_copies(g, b, par):
            cp.start()

    @pl.when(step + 1 < nsteps)
    def _():
        nxt = step + 1
        for cp in span_copies(nxt // nb, nxt % nb, 1 - par):
            cp.start()

    for cp in span_copies(g, b, par):
        cp.wait()

    slot_all = slot_ref[0]
    lane = lax.broadcasted_iota(jnp.int32, (tb, span), 1)
    onehot = jnp.concatenate(
        [jnp.where(slot_all[:, e:e + 1] == span_start(g, b, e) + lane, 1.0, 0.0).astype(BF16) for e in range(ne)],
        axis=1)
    acc_ref[...] = _dot(onehot, buf_ref[par].reshape(ne * span, buf_ref.shape[3]))

    lane_w = lax.broadcasted_iota(jnp.int32, (tb, LANES), 1)
    for e in range(ne):
        _, s1 = bounds(g, b, e)
        done = span_start(g, b, e) + span
        slot_col = slot_all[:, e:e + 1]

        def window(w, carry):
            want = done + w * LANES
            row0 = pl.multiple_of(jnp.minimum(want, cap - LANES), BF16_ROWS)
            cp = pltpu.make_async_copy(ye_hbm.at[g, e, pl.ds(row0, LANES), :], xbuf_ref, xsem_ref.at[0])
            cp.start()
            cp.wait()
            hit = (slot_col == row0 + lane_w) & (slot_col >= want)
            acc_ref[...] += _dot(jnp.where(hit, 1.0, 0.0).astype(BF16), xbuf_ref[...])
            return carry

        lax.fori_loop(0, (jnp.maximum(s1 - done, 0) + LANES - 1) // LANES, window, 0)

    m = mod_ref[0]
    xn = x_ref[...] + m[5:6] * acc_ref[...]
    if final:
        y = _rms(xn, fw_ref[...])
        for gi, ref in enumerate(o_refs):

            @pl.when(g == gi)
            def _():
                ref[...] = y
    else:
        o_refs[0][...] = xn


def _combine(lay, starts, x, slot_t, mod, final_w, ye, tb, final):
    nt, d = x.shape
    ng, n, ne = slot_t.shape
    cap = ye.shape[2]
    nb = n // tb
    span = min(LANES, cap)
    if final:
        out_specs = [pl.BlockSpec((tb, d),
                                  lambda g, b, st, gi=gi: (jnp.where(g == gi, b, jnp.where(g > gi, nb - 1, 0)), 0))
                     for gi in range(ng)]
        out_shape = [jax.ShapeDtypeStruct((n, d), F32) for _ in range(ng)]
    else:
        out_specs = [pl.BlockSpec((tb, d), lambda g, b, st: (g * nb + b, 0))]
        out_shape = [jax.ShapeDtypeStruct((nt, d), F32)]
    grid_spec = pltpu.PrefetchScalarGridSpec(
        num_scalar_prefetch=1,
        grid=(ng, nb),
        in_specs=[pl.BlockSpec((tb, d), lambda g, b, st: (g * nb + b, 0)),
                  pl.BlockSpec((1, tb, ne), lambda g, b, st: (g, b, 0)),
                  pl.BlockSpec((1, N_MOD, d), lambda g, b, st: (lay.seq_index((g * nb + b) * tb), 0, 0)),
                  pl.BlockSpec((1, d), lambda g, b, st: (0, 0)),
                  pl.BlockSpec(memory_space=pl.ANY)],
        out_specs=out_specs,
        scratch_shapes=[pltpu.VMEM((2, ne, span, d), BF16), pltpu.SemaphoreType.DMA((2, ne)),
                        pltpu.VMEM((LANES, d), BF16), pltpu.SemaphoreType.DMA((1,)),
                        pltpu.VMEM((tb, d), F32)])
    return pl.pallas_call(
        functools.partial(_combine_body, nb=nb, tb=tb, ne=ne, span=span, cap=cap, final=final),
        grid_spec=grid_spec,
        out_shape=out_shape,
        compiler_params=_cparams("arbitrary", "arbitrary"),
        name="ec_combine",
    )(starts, x, slot_t, mod, final_w, ye)


def _block_starts(rank, tb, cap):
    ng, ne = rank.shape[:2]
    first = rank[:, :, ::tb // LANES, 0]
    return jnp.concatenate([first, jnp.full((ng, ne, 1), cap, jnp.int32)], axis=2).reshape(-1)


def kernel(x_prompt, x_sample, c_prompt, c_sample, norm1_w, norm2_w, w_mod, b_mod, w_in, b_in, mlstm_norm_w, na_rpb, conv_w, conv_b, lru_wa, lru_ba, lru_wx, lru_bx, lru_L, w_br_a, w_br_b, w_br_c, w_out, w_router, b_router, w_gate_e, w_up_e, w_down_e, final_norm_w):
    bp, tp, d = x_prompt.shape
    bs, ts, _ = x_sample.shape
    lay = Layout(bp, tp, bs, ts, d)
    assert lay.n_p == lay.n_s, "the two request groups are routed as equal-sized token sets"
    depth = w_in.shape[0]
    ne = w_router.shape[2]
    ng, n = 2, lay.n_p
    cap = EC_FACTOR * n // ne
    tm_in = min(1024, ts)
    tn_in = 3 * d // 2
    tm_merge = min(512, ts)
    tt_lru = min(256, ts)
    tb_gather = min(2048, n)
    tsub_gather = min(512, n)
    tb_comb = min(256, ts)
    rt = min(512, cap)

    x = jnp.concatenate([x_prompt.reshape(lay.n_p, d), x_sample.reshape(lay.n_s, d)], axis=0)
    nseq = bp + bs
    rows_c = -(-nseq // SUBLANES) * SUBLANES
    c_all = jnp.zeros((rows_c, d), F32).at[:nseq].set(jnp.concatenate([c_prompt, c_sample], axis=0))
    mod_all = _modulation(c_all, w_mod, b_mod).reshape(depth, rows_c, N_MOD, d)

    widths = (d, d, d, d, 4 * MLSTM_HEADS, d, d, d, d, d, N_BRANCH * d)
    offs = [0]
    for w in widths:
        offs.append(offs[-1] + w)
    col = lambda a, i: a[..., offs[i]:offs[i + 1]]
    bcols = (0, 1, 2, 5, 6, 7)
    fcols = (3, 8, 9, 10)

    for l in range(depth):
        wl, bl = w_in[l], b_in[l]
        wb = jnp.concatenate([col(wl, i) for i in bcols], axis=1).astype(BF16)
        bb = jnp.concatenate([col(bl, i) for i in bcols], axis=0).reshape(1, -1)
        wf = jnp.concatenate([col(wl, i) for i in fcols], axis=1).astype(BF16)
        bf = jnp.concatenate([col(bl, i) for i in fcols], axis=0).reshape(1, -1)
        ngate = 4 * MLSTM_HEADS
        wg = jnp.zeros((d, LANES), BF16).at[:, :ngate].set(col(wl, 4).astype(BF16))
        bg = jnp.zeros((1, LANES), F32).at[0, :ngate].set(col(bl, 4))
        mod = mod_all[l]

        proj_b, proj_f, gates = _input_projection(lay, x, mod, norm1_w[l].reshape(1, d), wb, bb, wf, bf, wg, bg, tm_in,
                                                   tn_in)
        gates_t = gates[:, :ngate].T

        h_fwd, h_rev = _mlstm(lay, proj_b, gates, gates_t)

        bias = _na_bias_table(na_rpb[l])
        hb_p = _neigh_attn(proj_b, bias, d, 0, bp, tp, 3)
        hb_s = _neigh_attn(proj_b, bias, d, lay.n_p, bs, ts, 3)

        lru_args = (conv_w[l], conv_b[l].reshape(1, d))
        l_fwd = _lru_pass(lay, proj_f, *lru_args, _lru_gate_weights(lru_wa[l, 0], lru_wx[l, 0]),
                          lru_ba[l, 0].reshape(1, d), lru_bx[l, 0].reshape(1, d), lru_L[l, 0].reshape(1, d),
                          None, rev=False, tt=tt_lru)
        h_c = _lru_pass(lay, proj_f, *lru_args, _lru_gate_weights(lru_wa[l, 1], lru_wx[l, 1]),
                        lru_ba[l, 1].reshape(1, d), lru_bx[l, 1].reshape(1, d), lru_L[l, 1].reshape(1, d),
                        l_fwd, rev=True, tt=tt_lru)

        x, xm2, logits_t = _merge_project(
            lay, h_fwd, h_rev, mlstm_norm_w[l].reshape(1, d), hb_p, hb_s, h_c, proj_f, x, mod,
            w_br_a[l].astype(BF16), w_br_b[l].astype(BF16),
            w_br_c[l].astype(BF16), w_out[l].astype(BF16), norm2_w[l].reshape(1, d),
            w_router[l].T.astype(BF16), b_router[l].reshape(ne, 1), tm_merge)

        slot, rank, aff = _route(logits_t, ng, cap)
        ye = _expert_ffn(_block_starts(rank, tsub_gather, cap), xm2.reshape(ng, n, d), slot, aff,
                         w_gate_e[l].astype(BF16), w_up_e[l].astype(BF16), w_down_e[l].astype(BF16),
                         cap, tb_gather, tsub_gather, rt)
        slot_t = slot.reshape(ng, ne, n).transpose(0, 2, 1)
        outs = _combine(lay, _block_starts(rank, tb_comb, cap), x, slot_t, mod, final_norm_w.reshape(1, d), ye,
                        tb_comb, final=(l == depth - 1))
        x = outs[0]

    return (outs[0].reshape(bp, tp, d), outs[1].reshape(bs, ts, d))
```

```python
import functools
from typing import NamedTuple

import jax
import jax.numpy as jnp
from jax import lax
from jax.experimental import pallas as pl
from jax.experimental.pallas import tpu as pltpu

EPS = 1e-6
NEG_INF = -1e30
GRID_W = 64
MLSTM_HEADS = 4
MLSTM_CHUNK = 128
NA_HEADS = 16
NA_KH = 8
NA_KW = 16
LRU_BLOCKS = 16
LRU_C = 8.0
CONV_W = 4
N_BRANCH = 3
N_EXPERTS = 16
EC_FACTOR = 2
N_MOD = 6

LANES = 128
SUBLANES = 8
BF16_ROWS = 16
MXU_DIM = 256
VMEM_LIMIT = 56 * 1024 * 1024

F32 = jnp.float32
BF16 = jnp.bfloat16


class Layout(NamedTuple):
    bp: int
    tp: int
    bs: int
    ts: int
    d: int

    @property
    def n_p(self):
        return self.bp * self.tp

    @property
    def n_s(self):
        return self.bs * self.ts

    @property
    def nt(self):
        return self.n_p + self.n_s

    def seq_index(self, tok0):
        return jnp.where(tok0 < self.n_p, tok0 // self.tp, self.bp + (tok0 - self.n_p) // self.ts)

    def seq_len(self, tok0):
        return jnp.where(tok0 < self.n_p, self.tp, self.ts)

    def pos_in_seq(self, tok0):
        return jnp.where(tok0 < self.n_p, tok0 % self.tp, (tok0 - self.n_p) % self.ts)


def _cparams(*sem):
    return pltpu.CompilerParams(dimension_semantics=sem, vmem_limit_bytes=VMEM_LIMIT)


def _dot(a, b):
    return jnp.dot(a, b, preferred_element_type=F32)


def _dot_nt(a, b):
    return lax.dot_general(a, b, (((1,), (1,)), ((), ())), preferred_element_type=F32)


def _dot_tn(a, b):
    return lax.dot_general(a, b, (((0,), (0,)), ((), ())), preferred_element_type=F32)


def _sigmoid(x):
    return 1.0 / (1.0 + jnp.exp(-x))


def _rms(x, w):
    return x * lax.rsqrt(jnp.mean(x * x, axis=-1, keepdims=True) + EPS) * w


def _mod_body(c_ref, w_ref, b_ref, o_ref):
    c = c_ref[...]
    ca = (c * _sigmoid(c)).astype(BF16)
    o_ref[0] = _dot(ca, w_ref[0].astype(BF16)) + b_ref[0]


def _modulation(c_all, w_mod, b_mod):
    depth, d, nm = w_mod.shape
    rows = c_all.shape[0]
    tn = 1536
    return pl.pallas_call(
        _mod_body,
        grid=(depth, nm // tn),
        in_specs=[pl.BlockSpec((rows, d), lambda l, j: (0, 0)),
                  pl.BlockSpec((1, d, tn), lambda l, j: (l, 0, j)),
                  pl.BlockSpec((1, 1, tn), lambda l, j: (l, 0, j))],
        out_specs=pl.BlockSpec((1, rows, tn), lambda l, j: (l, 0, j)),
        out_shape=jax.ShapeDtypeStruct((depth, rows, nm), F32),
        compiler_params=_cparams("arbitrary", "arbitrary"),
        name="adaln_modulation",
    )(c_all, w_mod, b_mod.reshape(depth, 1, nm))


def _inproj_body(x_ref, mod_ref, nw_ref, wb_ref, bb_ref, wf_ref, bf_ref, wg_ref, bg_ref,
                 ob_ref, of_ref, og_ref, xm_ref, *, n_bf):
    j = pl.program_id(1)

    @pl.when(j == 0)
    def _():
        m = mod_ref[0]
        xm = (_rms(x_ref[...], nw_ref[...]) * (1.0 + m[1:2]) + m[0:1]).astype(BF16)
        xm_ref[...] = xm
        og_ref[...] = _dot(xm, wg_ref[...]) + bg_ref[...]

    @pl.when(j < n_bf)
    def _():
        ob_ref[...] = (_dot(xm_ref[...], wb_ref[...]) + bb_ref[...]).astype(BF16)

    @pl.when(j >= n_bf)
    def _():
        of_ref[...] = (_dot(xm_ref[...], wf_ref[...]) + bf_ref[...]).astype(of_ref.dtype)


def _input_projection(lay, x, mod, norm_w, wb, bb, wf, bf, wg, bg, tm, tn):
    nt, d = x.shape
    n_bf, n_f = wb.shape[1] // tn, wf.shape[1] // tn
    seq = lambda i: lay.seq_index(i * tm)
    jb = lambda j: jnp.minimum(j, n_bf - 1)
    jf = lambda j: jnp.maximum(j - n_bf, 0)
    return pl.pallas_call(
        functools.partial(_inproj_body, n_bf=n_bf),
        grid=(nt // tm, n_bf + n_f),
        in_specs=[pl.BlockSpec((tm, d), lambda i, j: (i, 0)),
                  pl.BlockSpec((1, N_MOD, d), lambda i, j: (seq(i), 0, 0)),
                  pl.BlockSpec((1, d), lambda i, j: (0, 0)),
                  pl.BlockSpec((d, tn), lambda i, j: (0, jb(j))),
                  pl.BlockSpec((1, tn), lambda i, j: (0, jb(j))),
                  pl.BlockSpec((d, tn), lambda i, j: (0, jf(j))),
                  pl.BlockSpec((1, tn), lambda i, j: (0, jf(j))),
                  pl.BlockSpec((d, LANES), lambda i, j: (0, 0)),
                  pl.BlockSpec((1, LANES), lambda i, j: (0, 0))],
        out_specs=[pl.BlockSpec((tm, tn), lambda i, j: (i, jb(j))),
                   pl.BlockSpec((tm, tn), lambda i, j: (i, jf(j))),
                   pl.BlockSpec((tm, LANES), lambda i, j: (i, 0))],
        out_shape=[jax.ShapeDtypeStruct((nt, wb.shape[1]), BF16),
                   jax.ShapeDtypeStruct((nt, wf.shape[1]), BF16),
                   jax.ShapeDtypeStruct((nt, LANES), F32)],
        scratch_shapes=[pltpu.VMEM((tm, d), BF16)],
        compiler_params=_cparams("arbitrary", "arbitrary"),
        name="norm_inproj",
    )(x, mod, norm_w, wb, bb, wf, bf, wg, bg)


def _lane_cumsum(x, rev):
    n = x.shape[-1]
    lane = lax.broadcasted_iota(jnp.int32, x.shape, 1)
    k = 1
    while k < n:
        if rev:
            x = x + jnp.where(lane < n - k, pltpu.roll(x, n - k, 1), 0.0)
        else:
            x = x + jnp.where(lane >= k, pltpu.roll(x, k, 1), 0.0)
        k *= 2
    return x


def _log_sigmoid(x):
    return jnp.minimum(x, 0.0) - jnp.log(1.0 + jnp.exp(-jnp.abs(x)))


def _mlstm_body(qf_ref, kf_ref, vf_ref, grf_ref, gcf_ref, qb_ref, kb_ref, vb_ref, grb_ref, gcb_ref,
                hf_ref, hb_ref, c_ref, m_ref, wi_ref, col_ref, sc_ref, qc_ref, *, lay, heads, chunk):
    s = pl.program_id(0)
    tok_f = s * chunk
    tok_b = (pl.num_programs(0) - 1 - s) * chunk
    dh = qf_ref.shape[1] // heads

    def reset(lo):
        c_ref[lo:lo + heads] = jnp.zeros((heads,) + c_ref.shape[1:], F32)
        m_ref[lo:lo + heads] = jnp.zeros((heads,) + m_ref.shape[1:], F32)

    @pl.when(lay.pos_in_seq(tok_f) == 0)
    def _():
        reset(0)

    @pl.when(lay.pos_in_seq(tok_b) + chunk == lay.seq_len(tok_b))
    def _():
        reset(heads)

    sides = ((qf_ref, kf_ref, vf_ref, grf_ref, gcf_ref, hf_ref), (qb_ref, kb_ref, vb_ref, grb_ref, gcb_ref, hb_ref))
    units = [(rev, h) for rev in (0, 1) for h in range(heads)]
    ri = lax.broadcasted_iota(jnp.int32, (chunk, chunk), 0)
    cj = lax.broadcasted_iota(jnp.int32, (chunk, chunk), 1)
    kscale = dh ** -0.5
    ones = jnp.ones((chunk, LANES), BF16)
    rep = lambda col: jnp.broadcast_to(col, (chunk, LANES))

    decays, m_news = [], []
    for rev in (0, 1):
        gr = sides[rev][3][...]
        gc = sides[rev][4][...]
        gi = 2 * heads * rev
        lf_all = _log_sigmoid(gr)
        b_all = _lane_cumsum(lf_all, bool(rev))
        mask = (cj >= ri) if rev else (cj <= ri)
        for h in range(heads):
            u = rev * heads + h
            ig_row = gr[gi + h:gi + h + 1, :]
            lf_row = lf_all[gi + heads + h:gi + heads + h + 1, :]
            b_row = b_all[gi + heads + h:gi + heads + h + 1, :]
            ig_col = gc[:, gi + h:gi + h + 1]
            m_old = m_ref[u][:, 0:1]
            b_col = jnp.sum(jnp.where(mask, lf_row, 0.0), axis=1, keepdims=True)
            dmat = jnp.where(mask, b_col - b_row + ig_row, NEG_INF)
            g = b_col + m_old
            m_row = jnp.maximum(g, jnp.max(dmat, axis=1, keepdims=True))
            wi_ref[u] = jnp.exp(dmat - m_row) * kscale
            b_last = b_row[:, 0:1] if rev else b_row[:, chunk - 1:chunk]
            m_new = jnp.maximum(b_last + m_old, jnp.max(b_last - b_row + ig_row, axis=1, keepdims=True))
            col_ref[u, 0] = rep(jnp.exp(g - m_row))
            col_ref[u, 1] = rep(jnp.exp(-m_row))
            col_ref[u, 2] = rep(jnp.exp(b_last - b_col + ig_col - m_new) * kscale)
            decays.append(jnp.exp(b_last + m_old - m_new))
            m_news.append(m_new)

    for rev, h in units:
        u = rev * heads + h
        q = sides[rev][0][:, h * dh:(h + 1) * dh]
        sc_ref[u] = (_dot_nt(q, sides[rev][1][:, h * dh:(h + 1) * dh]) * wi_ref[u]).astype(BF16)
    for rev, h in units:
        u = rev * heads + h
        q = sides[rev][0][:, h * dh:(h + 1) * dh]
        qc_ref[u] = _dot(q, c_ref[u].astype(BF16))

    for rev, h in units:
        u = rev * heads + h
        sl = slice(h * dh, (h + 1) * dh)
        v_ext = jnp.concatenate([sides[rev][2][:, sl], ones], axis=1)
        w_inter = col_ref[u, 0]
        ne = _dot(sc_ref[u], v_ext) + jnp.concatenate([w_inter] * (dh // LANES + 1), axis=1) * qc_ref[u]
        den = jnp.maximum(jnp.abs(ne[:, dh:]), col_ref[u, 1])
        sides[rev][5][:, sl] = ne[:, :dh] / jnp.concatenate([den] * (dh // LANES), axis=1)

    for rev, h in units:
        u = rev * heads + h
        sl = slice(h * dh, (h + 1) * dh)
        v_ext = jnp.concatenate([sides[rev][2][:, sl], ones], axis=1)
        kw = sides[rev][1][:, sl].astype(F32) * jnp.concatenate([col_ref[u, 2]] * (dh // LANES), axis=1)
        c_ref[u] = decays[u] * c_ref[u] + _dot_tn(kw.astype(BF16), v_ext)
        m_ref[u] = jnp.broadcast_to(m_news[u], m_ref.shape[1:])


def _mlstm(lay, proj_b, gates, gates_t):
    nt, d = lay.nt, lay.d
    chunk, heads = MLSTM_CHUNK, MLSTM_HEADS
    nc = nt // chunk
    dh = d // heads
    nu = 2 * heads

    def side(cidx):
        return [pl.BlockSpec((chunk, d), lambda s: (cidx(s), 0)),
                pl.BlockSpec((chunk, d), lambda s: (cidx(s), 1)),
                pl.BlockSpec((chunk, d), lambda s: (cidx(s), 2)),
                pl.BlockSpec((4 * heads, chunk), lambda s: (0, cidx(s))),
                pl.BlockSpec((chunk, LANES), lambda s: (cidx(s), 0))]

    fwd = lambda s: s
    bwd = lambda s: nc - 1 - s
    args = [proj_b, proj_b, proj_b, gates_t, gates]
    return pl.pallas_call(
        functools.partial(_mlstm_body, lay=lay, heads=heads, chunk=chunk),
        grid=(nc,),
        in_specs=side(fwd) + side(bwd),
        out_specs=[pl.BlockSpec((chunk, d), lambda s: (fwd(s), 0)), pl.BlockSpec((chunk, d), lambda s: (bwd(s), 0))],
        out_shape=[jax.ShapeDtypeStruct((nt, d), F32), jax.ShapeDtypeStruct((nt, d), F32)],
        scratch_shapes=[pltpu.VMEM((nu, dh, dh + LANES), F32), pltpu.VMEM((nu, 1, LANES), F32),
                        pltpu.VMEM((nu, chunk, chunk), F32), pltpu.VMEM((nu, 3, chunk, LANES), F32),
                        pltpu.VMEM((nu, chunk, chunk), BF16), pltpu.VMEM((nu, chunk, dh + LANES), F32)],
        compiler_params=_cparams("arbitrary"),
        name="mlstm_bidir",
    )(*(args + args))


def _na_bias_table(rpb):
    h = rpb.shape[0]
    cols = jnp.arange(GRID_W)
    cs = jnp.clip(cols - NA_KW // 2, 0, GRID_W - NA_KW)
    col_in = (cols[None, :] >= cs[:, None]) & (cols[None, :] < cs[:, None] + NA_KW)
    dc_idx = jnp.clip(cols[None, :] - cols[:, None], -(NA_KW - 1), NA_KW - 1) + NA_KW - 1
    t = jnp.where(col_in[None, None], rpb[:, :, dc_idx], NEG_INF)
    d_idx = jnp.arange(NA_KH)[:, None] + jnp.arange(NA_KH)[None, :]
    tc = t[:, d_idx]
    tc = tc.transpose(0, 1, 3, 2, 4).reshape(h, NA_KH, GRID_W, NA_KH * GRID_W)
    return tc.reshape(h // 2, 2, NA_KH, GRID_W, NA_KH * GRID_W).transpose(0, 2, 1, 3, 4).reshape(
        h // 2, NA_KH, 2 * GRID_W, NA_KH * GRID_W).astype(F32)


def _na_body(q_ref, k_ref, v_ref, bias_ref, o_ref, s_ref, p_ref, l_ref, *, rows, rblk, dh):
    rb = pl.program_id(2)
    lane_q = lax.broadcasted_iota(jnp.int32, (GRID_W, 2 * dh), 1)
    qscale = dh ** -0.5

    starts, scores = [], []
    for r in range(rblk):
        rg = rb * rblk + r
        rs = jnp.clip(rg - NA_KH // 2, 0, rows - NA_KH)
        d0 = rs - rg + NA_KH - 1
        q = q_ref[r * GRID_W:(r + 1) * GRID_W, :].astype(F32) * qscale
        q2 = jnp.concatenate([jnp.where(lane_q < dh, q, 0.0), jnp.where(lane_q >= dh, q, 0.0)], axis=0).astype(BF16)
        k0 = pl.multiple_of(rs * GRID_W, GRID_W)
        starts.append(k0)
        s_ref[r] = _dot_nt(q2, k_ref[pl.ds(k0, NA_KH * GRID_W), :]) + bias_ref[0, d0]
    for r in range(rblk):
        s = s_ref[r]
        p = jnp.exp(s - jnp.max(s, axis=1, keepdims=True))
        l_ref[r] = jnp.broadcast_to(jnp.sum(p, axis=1, keepdims=True), l_ref.shape[1:])
        p_ref[r] = p.astype(BF16)
    for r in range(rblk):
        o = _dot(p_ref[r], v_ref[pl.ds(starts[r], NA_KH * GRID_W), :]) / l_ref[r]
        out = jnp.where(lane_q < dh, o[:GRID_W], o[GRID_W:])
        o_ref[r * GRID_W:(r + 1) * GRID_W, :] = out.astype(o_ref.dtype)


def _neigh_attn(proj_b, bias, d, tok_off, nseq, t, col0):
    rows = t // GRID_W
    rblk = min(16, rows)
    dh = d // NA_HEADS
    pairs = NA_HEADS // 2
    pw = 2 * dh
    per = d // pw
    seq0 = tok_off // t
    qblk = rblk * GRID_W
    q0 = tok_off // qblk
    nrb = rows // rblk
    return pl.pallas_call(
        functools.partial(_na_body, rows=rows, rblk=rblk, dh=dh),
        grid=(pairs, nseq, nrb),
        in_specs=[pl.BlockSpec((qblk, pw), lambda p, b, r: (q0 + b * nrb + r, col0 * per + p)),
                  pl.BlockSpec((t, pw), lambda p, b, r: (seq0 + b, (col0 + 1) * per + p)),
                  pl.BlockSpec((t, pw), lambda p, b, r: (seq0 + b, (col0 + 2) * per + p)),
                  pl.BlockSpec((1, NA_KH, 2 * GRID_W, NA_KH * GRID_W), lambda p, b, r: (p, 0, 0, 0))],
        out_specs=pl.BlockSpec((qblk, pw), lambda p, b, r: (b * nrb + r, p)),
        out_shape=jax.ShapeDtypeStruct((nseq * t, d), BF16),
        scratch_shapes=[pltpu.VMEM((rblk, 2 * GRID_W, NA_KH * GRID_W), F32),
                        pltpu.VMEM((rblk, 2 * GRID_W, NA_KH * GRID_W), BF16),
                        pltpu.VMEM((rblk, 2 * GRID_W, LANES), F32)],
        compiler_params=_cparams("arbitrary", "arbitrary", "arbitrary"),
        name="neigh_attn",
    )(proj_b, proj_b, proj_b, bias)


def _softplus(x):
    return jnp.maximum(x, 0.0) + jnp.log(1.0 + jnp.exp(-jnp.abs(x)))


def _gelu_tanh(x):
    return 0.5 * x * (1.0 + jnp.tanh(0.7978845608028654 * (x + 0.044715 * (x * x * x))))


def _lru_body(*refs, lay, rev, tt):
    if rev:
        (x_ref, xp_ref, xn_ref, cw_ref, cb_ref, w_ref, ba_ref, bx_ref, lam_ref, hf_ref, y_ref,
         out_ref, carry_ref) = refs
    else:
        x_ref, xp_ref, xn_ref, cw_ref, cb_ref, w_ref, ba_ref, bx_ref, lam_ref, out_ref, carry_ref = refs
    s = pl.program_id(0)
    ti = (pl.num_programs(0) - 1 - s) if rev else s
    tok0 = ti * tt
    pos = lay.pos_in_seq(tok0)
    slen = lay.seq_len(tok0)
    is_first = pos == 0
    is_last = pos + tt == slen

    @pl.when(is_last if rev else is_first)
    def _():
        carry_ref[...] = jnp.zeros_like(carry_ref)

    d = x_ref.shape[1]
    x = x_ref[...].astype(F32)
    halo = SUBLANES
    xprev = jnp.where(is_first, 0.0, xp_ref[BF16_ROWS - halo:, :].astype(F32))
    xnext = jnp.where(is_last, 0.0, xn_ref[:halo, :].astype(F32))
    xe = jnp.concatenate([xprev, x, xnext], axis=0)
    pad = CONV_W // 2
    xc = cb_ref[...]
    for j in range(CONV_W):
        o = halo - pad + j
        xc = xc + xe[o:o + tt] * cw_ref[j:j + 1, :]
    sp = _softplus(-lam_ref[...])
    ng = tt // SUBLANES
    sub = lax.broadcasted_iota(jnp.int32, (ng, SUBLANES, MXU_DIM), 1)
    for t in range(d // MXU_DIM):
        cs = slice(t * MXU_DIM, (t + 1) * MXU_DIM)
        xct = xc[:, cs]
        gg = _dot(xct.astype(BF16), w_ref[t])
        r = _sigmoid(gg[:, :MXU_DIM] + ba_ref[:, cs])
        i = _sigmoid(gg[:, MXU_DIM:] + bx_ref[:, cs])
        log_a = -LRU_C * r * sp[:, cs]
        a = jnp.exp(log_a)
        bt = jnp.sqrt(1.0 - a * a) * (i * xct)
        a3 = a.reshape(ng, SUBLANES, MXU_DIM)
        b3 = bt.reshape(ng, SUBLANES, MXU_DIM)
        k = 1
        while k < SUBLANES:
            if rev:
                ok = sub < SUBLANES - k
                ash = pltpu.roll(a3, SUBLANES - k, 1)
                bsh = pltpu.roll(b3, SUBLANES - k, 1)
            else:
                ok = sub >= k
                ash = pltpu.roll(a3, k, 1)
                bsh = pltpu.roll(b3, k, 1)
            b3 = jnp.where(ok, a3 * bsh + b3, b3)
            a3 = jnp.where(ok, a3 * ash, a3)
            k *= 2
        hprev = carry_ref[:, cs]
        outs = [None] * ng
        order = range(ng - 1, -1, -1) if rev else range(ng)
        for gidx in order:
            hg = a3[gidx] * hprev + b3[gidx]
            outs[gidx] = hg
            hprev = hg[0:1] if rev else hg[SUBLANES - 1:SUBLANES]
        carry_ref[:, cs] = hprev
        hh = jnp.concatenate(outs, axis=0)
        if rev:
            out_ref[:, cs] = ((hf_ref[:, cs] + hh) * _gelu_tanh(y_ref[:, cs].astype(F32))).astype(out_ref.dtype)
        else:
            out_ref[:, cs] = hh


def _lru_pass(lay, proj_f, conv_w, conv_b, w_dir, ba, bx, lam, h_fwd, rev, tt):
    nt, d = lay.nt, lay.d
    ntile = nt // tt
    hb = tt // BF16_ROWS
    nhb = nt // BF16_ROWS
    tidx = (lambda s: ntile - 1 - s) if rev else (lambda s: s)
    xcol = 1
    in_specs = [pl.BlockSpec((tt, d), lambda s: (tidx(s), xcol)),
                pl.BlockSpec((BF16_ROWS, d), lambda s: (jnp.maximum(tidx(s) * hb - 1, 0), xcol)),
                pl.BlockSpec((BF16_ROWS, d), lambda s: (jnp.minimum((tidx(s) + 1) * hb, nhb - 1), xcol)),
                pl.BlockSpec((CONV_W, d), lambda s: (0, 0)),
                pl.BlockSpec((1, d), lambda s: (0, 0)),
                pl.BlockSpec((d // MXU_DIM, MXU_DIM, 2 * MXU_DIM), lambda s: (0, 0, 0)),
                pl.BlockSpec((1, d), lambda s: (0, 0)),
                pl.BlockSpec((1, d), lambda s: (0, 0)),
                pl.BlockSpec((1, d), lambda s: (0, 0))]
    args = [proj_f, proj_f, proj_f, conv_w, conv_b, w_dir, ba, bx, lam]
    if rev:
        in_specs += [pl.BlockSpec((tt, d), lambda s: (tidx(s), 0)),
                     pl.BlockSpec((tt, d), lambda s: (tidx(s), 2))]
        args += [h_fwd, proj_f]
    return pl.pallas_call(
        functools.partial(_lru_body, lay=lay, rev=rev, tt=tt),
        grid=(ntile,),
        in_specs=in_specs,
        out_specs=pl.BlockSpec((tt, d), lambda s: (tidx(s), 0)),
        out_shape=jax.ShapeDtypeStruct((nt, d), BF16 if rev else F32),
        scratch_shapes=[pltpu.VMEM((1, d), F32)],
        compiler_params=_cparams("arbitrary"),
        name="rglru_bwd" if rev else "rglru_fwd",
    )(*args)


def _lru_gate_weights(wa, wx):
    nb, bw, _ = wa.shape
    per = MXU_DIM // bw
    nt = nb // per

    def diag(w):
        w = w.reshape(nt, per, bw, bw)
        eye = jnp.eye(per, dtype=w.dtype)
        return jnp.einsum('tpkj,pq->tpkqj', w, eye).reshape(nt, MXU_DIM, MXU_DIM)

    return jnp.concatenate([diag(wa), diag(wx)], axis=-1).astype(BF16)


def _merge_body(hf_ref, hr_ref, o_ref, hnw_ref, hbp_ref, hbs_ref, hc_ref, g0_ref, g1_ref, g2_ref, x_ref, mod_ref,
                wa_ref, wb_ref, wc_ref, wo_ref, nw_ref, wr_ref, br_ref, xo_ref, xm_ref, lg_ref, *, heads, np_tiles):
    m = mod_ref[0]
    h_b = jnp.where(pl.program_id(0) < np_tiles, hbp_ref[...], hbs_ref[...])
    dh = hf_ref.shape[1] // heads
    parts = []
    for h in range(heads):
        sl = slice(h * dh, (h + 1) * dh)
        hs = hf_ref[:, sl] + hr_ref[:, sl]
        mu = jnp.mean(hs, axis=1, keepdims=True)
        var = jnp.mean(jnp.square(hs - mu), axis=1, keepdims=True)
        hn = (hs - mu) * lax.rsqrt(var + EPS) * hnw_ref[:, sl]
        parts.append((_sigmoid(o_ref[:, sl].astype(F32)) * hn).astype(BF16))
    h_a = jnp.concatenate(parts, axis=1)
    gate = lambda ref: _sigmoid(ref[...].astype(F32))
    merged = (gate(g0_ref) * _dot(h_a, wa_ref[...]) + gate(g1_ref) * _dot(h_b, wb_ref[...])
              + gate(g2_ref) * _dot(hc_ref[...], wc_ref[...]))
    xn = x_ref[...] + m[2:3] * _dot(merged.astype(BF16), wo_ref[...])
    xo_ref[...] = xn
    xm = (_rms(xn, nw_ref[...]) * (1.0 + m[4:5]) + m[3:4]).astype(BF16)
    xm_ref[...] = xm
    lg_ref[...] = _dot_nt(wr_ref[...], xm) + br_ref[...]


def _merge_project(lay, h_fwd, h_rev, mlstm_nw, hb_p, hb_s, h_c, proj_f, x, mod, wa, wb, wc, wo, norm_w, wr_t, br_t,
                   tm):
    nt, d = x.shape
    ne = wr_t.shape[0]
    seq = lambda i: lay.seq_index(i * tm)
    act = lambda c: pl.BlockSpec((tm, d), lambda i: (i, c))
    const = lambda shape: pl.BlockSpec(shape, lambda i: (0,) * len(shape), pipeline_mode=pl.Buffered(1))
    np_tiles = lay.n_p // tm
    return pl.pallas_call(
        functools.partial(_merge_body, heads=MLSTM_HEADS, np_tiles=np_tiles),
        grid=(nt // tm,),
        in_specs=[act(0), act(0), act(0), const((1, d)),
                  pl.BlockSpec((tm, d), lambda i: (jnp.minimum(i, np_tiles - 1), 0)),
                  pl.BlockSpec((tm, d), lambda i: (jnp.maximum(i - np_tiles, 0), 0)),
                  act(0), act(3), act(4), act(5), act(0),
                  pl.BlockSpec((1, N_MOD, d), lambda i: (seq(i), 0, 0)),
                  const((d, d)), const((d, d)), const((d, d)), const((d, d)),
                  const((1, d)), const((ne, d)), const((ne, 1))],
        out_specs=[pl.BlockSpec((tm, d), lambda i: (i, 0)),
                   pl.BlockSpec((tm, d), lambda i: (i, 0)),
                   pl.BlockSpec((ne, tm), lambda i: (0, i))],
        out_shape=[jax.ShapeDtypeStruct((nt, d), F32),
                   jax.ShapeDtypeStruct((nt, d), BF16),
                   jax.ShapeDtypeStruct((ne, nt), F32)],
        compiler_params=_cparams("arbitrary"),
        name="merge_outproj",
    )(h_fwd, h_rev, proj_f, mlstm_nw, hb_p, hb_s, h_c, proj_f, proj_f, proj_f, x, mod, wa, wb, wc, wo, norm_w, wr_t, br_t)


def _prefix_count(flags, tri_u, tri_l):
    ne, nr, nl = flags.shape
    fb = flags.astype(BF16)
    inc = _dot(fb.reshape(ne * nr, nl), tri_u).reshape(ne, nr, nl)
    tot = jnp.broadcast_to(inc[:, :, nl - 1:nl], (ne, nr, nl)).astype(BF16)
    offs = jnp.stack([_dot(tri_l, tot[e]) for e in range(ne)], axis=0)
    return offs + inc - flags


def _route_body(lg_ref, slot_ref, rank_ref, aff_ref, *, cap):
    lg = lg_ref[0]
    ne, nr, nl = lg.shape
    ex = jnp.exp(lg - jnp.max(lg, axis=0, keepdims=True))
    aff = ex / jnp.sum(ex, axis=0, keepdims=True)
    bits = pltpu.bitcast(aff, jnp.int32)

    def count(flags):
        return jnp.sum(jnp.sum(flags, axis=1, keepdims=True), axis=2, keepdims=True)

    def bisect(i, thr):
        cand = thr | lax.shift_left(jnp.int32(1), 30 - i)
        cnt = count(jnp.where(bits >= cand, 1.0, 0.0))
        return jnp.where(cnt >= cap, cand, thr)

    thr = lax.fori_loop(0, 31, bisect, jnp.zeros((ne, 1, 1), jnp.int32))
    li = lax.broadcasted_iota(jnp.int32, (nl, nl), 0)
    lj = lax.broadcasted_iota(jnp.int32, (nl, nl), 1)
    tri_u = jnp.where(li <= lj, 1.0, 0.0).astype(BF16)
    ri = lax.broadcasted_iota(jnp.int32, (nr, nr), 0)
    rj = lax.broadcasted_iota(jnp.int32, (nr, nr), 1)
    tri_l = jnp.where(rj < ri, 1.0, 0.0).astype(BF16)
    gt = jnp.where(bits > thr, 1.0, 0.0)
    eq = jnp.where(bits == thr, 1.0, 0.0)
    need = cap - count(gt)
    sel = gt + eq * jnp.where(_prefix_count(eq, tri_u, tri_l) < need, 1.0, 0.0)
    rank = _prefix_count(sel, tri_u, tri_l).astype(jnp.int32)
    rank_ref[0] = rank
    slot_ref[0] = jnp.where(sel > 0.5, rank, -1)
    aff_ref[0] = aff


def _route(logits_t, ng, cap):
    ne, nall = logits_t.shape
    n = nall // ng
    nr = n // LANES
    lg = logits_t.reshape(ne, ng, nr, LANES).transpose(1, 0, 2, 3)
    spec = pl.BlockSpec((1, ne, nr, LANES), lambda g: (g, 0, 0, 0))
    return pl.pallas_call(
        functools.partial(_route_body, cap=cap),
        grid=(ng,),
        in_specs=[spec],
        out_specs=[spec, spec, spec],
        out_shape=[jax.ShapeDtypeStruct((ng, ne, nr, LANES), jnp.int32),
                   jax.ShapeDtypeStruct((ng, ne, nr, LANES), jnp.int32),
                   jax.ShapeDtypeStruct((ng, ne, nr, LANES), F32)],
        compiler_params=_cparams("arbitrary"),
        name="ec_route",
    )(lg)


def _window_count(s0, s1):
    return jnp.where(s1 > s0, (s1 - 1) // LANES - s0 // LANES + 1, 0)


def _expert_body(st_ref, x_ref, slot_ref, aff_ref, wg_ref, wu_ref, wd_ref, ye_ref, xe_ref, gate_ref,
                 *, nb, nsub, span, cap, rt, fc):
    e, g, s = pl.program_id(0), pl.program_id(1), pl.program_id(2)
    ne = pl.num_programs(0)

    @pl.when(s == 0)
    def _():
        xe_ref[...] = jnp.zeros_like(xe_ref)
        gate_ref[...] = jnp.zeros_like(gate_ref)

    @pl.when(s < nb)
    def _():
        tsub = x_ref.shape[1] // nsub
        rsub = tsub // LANES
        sub = lax.broadcasted_iota(jnp.int32, (span, LANES), 0)

        def gather(row0, want, slot, aff, xs):
            tgt = sub + row0
            hits = [slot[c:c + 1, :] == tgt for c in range(rsub)]
            if want is not None:
                hits = [h & (slot[c:c + 1, :] >= want) for c, h in enumerate(hits)]
            onehot = jnp.concatenate([jnp.where(h, 1.0, 0.0).astype(BF16) for h in hits], axis=1)
            xe_ref[pl.ds(row0, span), :] += _dot(onehot, xs).astype(BF16)
            gsum = functools.reduce(lambda a, b: a + b,
                                    [jnp.where(h, aff[c:c + 1, :], 0.0) for c, h in enumerate(hits)])
            gate_ref[pl.ds(row0, span), :] += jnp.sum(gsum, axis=1, keepdims=True)

        for j in range(nsub):
            base = (g * ne + e) * (nb * nsub + 1) + s * nsub + j
            s0, s1 = st_ref[base], st_ref[base + 1]
            slot = slot_ref[0, 0, j * rsub:(j + 1) * rsub, :]
            aff = aff_ref[0, 0, j * rsub:(j + 1) * rsub, :]
            xs = x_ref[0, j * tsub:(j + 1) * tsub, :]
            first = pl.multiple_of(jnp.minimum((s0 // BF16_ROWS) * BF16_ROWS, cap - span), BF16_ROWS)
            gather(first, None, slot, aff, xs)

            def extra(w, carry):
                want = first + span + w * span
                row0 = pl.multiple_of(jnp.minimum(want, cap - span), BF16_ROWS)
                gather(row0, want, slot, aff, xs)
                return carry

            lax.fori_loop(0, (jnp.maximum(s1 - first - span, 0) + span - 1) // span, extra, 0)

    @pl.when(s >= nb)
    def _():
        r = s - nb
        rows = pl.ds(pl.multiple_of(r * rt, rt), rt)
        xe = xe_ref[rows, :]
        f = wg_ref.shape[2]
        acc = None
        for c in range(f // fc):
            fs = slice(c * fc, (c + 1) * fc)
            gg = _dot(xe, wg_ref[0, :, fs].astype(BF16))
            hid = (gg * _sigmoid(gg) * _dot(xe, wu_ref[0, :, fs].astype(BF16))).astype(BF16)
            part = _dot(hid, wd_ref[0, fs, :].astype(BF16))
            acc = part if acc is None else acc + part
        ye_ref[0, 0] = (acc * gate_ref[rows, :]).astype(ye_ref.dtype)


def _expert_ffn(starts, xm, slot, aff, wg, wu, wd, cap, tb, tsub, rt):
    ng, n, d = xm.shape
    ne, _, f = wg.shape
    nb = n // tb
    nr = cap // rt
    span = min(LANES, cap)
    tok = lambda e, g, s, st: (g, e, jnp.minimum(s, nb - 1), 0)
    weight = lambda shape: pl.BlockSpec(shape, lambda e, g, s, st: (e, 0, 0), pipeline_mode=pl.Buffered(1))
    grid_spec = pltpu.PrefetchScalarGridSpec(
        num_scalar_prefetch=1,
        grid=(ne, ng, nb + nr),
        in_specs=[pl.BlockSpec((1, tb, d), lambda e, g, s, st: (g, jnp.minimum(s, nb - 1), 0)),
                  pl.BlockSpec((1, 1, tb // LANES, LANES), tok),
                  pl.BlockSpec((1, 1, tb // LANES, LANES), tok),
                  weight((1, d, f)), weight((1, d, f)), weight((1, f, d))],
        out_specs=pl.BlockSpec((1, 1, rt, d), lambda e, g, s, st: (g, e, jnp.maximum(s - nb, 0), 0)),
        scratch_shapes=[pltpu.VMEM((cap, d), BF16), pltpu.VMEM((cap, 1), F32)])
    return pl.pallas_call(
        functools.partial(_expert_body, nb=nb, nsub=tb // tsub, span=span, cap=cap, rt=rt, fc=min(512, f)),
        grid_spec=grid_spec,
        out_shape=jax.ShapeDtypeStruct((ng, ne, cap, d), BF16),
        compiler_params=_cparams("arbitrary", "arbitrary", "arbitrary"),
        name="ec_expert_ffn",
    )(starts, xm, slot, aff, wg, wu, wd)


def _combine_body(st_ref, x_ref, slot_ref, mod_ref, fw_ref, ye_hbm, *rest, nb, tb, ne, span, cap, final):
    o_refs = rest[:-5]
    buf_ref, sem_ref, xbuf_ref, xsem_ref, acc_ref = rest[-5:]
    g, b = pl.program_id(0), pl.program_id(1)
    step = g * nb + b
    nsteps = pl.num_programs(0) * nb
    par = step % 2

    def bounds(gg, bb, e):
        base = (gg * ne + e) * (nb + 1) + bb
        return st_ref[base], st_ref[base + 1]

    def span_start(gg, bb, e):
        s0, _ = bounds(gg, bb, e)
        return pl.multiple_of(jnp.minimum((s0 // BF16_ROWS) * BF16_ROWS, cap - span), BF16_ROWS)

    def span_copies(gg, bb, parity):
        return [pltpu.make_async_copy(ye_hbm.at[gg, e, pl.ds(span_start(gg, bb, e), span), :],
                                      buf_ref.at[parity, e], sem_ref.at[parity, e]) for e in range(ne)]

    @pl.when(step == 0)
    def _():
        for cp in span_copies(g, b, par):
            cp.start()

    @pl.when(step + 1 < nsteps)
    def _():
        nxt = step + 1
        for cp in span_copies(nxt // nb, nxt % nb, 1 - par):
            cp.start()

    for cp in span_copies(g, b, par):
        cp.wait()

    slot_all = slot_ref[0]
    lane = lax.broadcasted_iota(jnp.int32, (tb, span), 1)
    onehot = jnp.concatenate(
        [jnp.where(slot_all[:, e:e + 1] == span_start(g, b, e) + lane, 1.0, 0.0).astype(BF16) for e in range(ne)],
        axis=1)
    acc_ref[...] = _dot(onehot, buf_ref[par].reshape(ne * span, buf_ref.shape[3]))

    lane_w = lax.broadcasted_iota(jnp.int32, (tb, LANES), 1)
    for e in range(ne):
        _, s1 = bounds(g, b, e)
        done = span_start(g, b, e) + span
        slot_col = slot_all[:, e:e + 1]

        def window(w, carry):
            want = done + w * LANES
            row0 = pl.multiple_of(jnp.minimum(want, cap - LANES), BF16_ROWS)
            cp = pltpu.make_async_copy(ye_hbm.at[g, e, pl.ds(row0, LANES), :], xbuf_ref, xsem_ref.at[0])
            cp.start()
            cp.wait()
            hit = (slot_col == row0 + lane_w) & (slot_col >= want)
            acc_ref[...] += _dot(jnp.where(hit, 1.0, 0.0).astype(BF16), xbuf_ref[...])
            return carry

        lax.fori_loop(0, (jnp.maximum(s1 - done, 0) + LANES - 1) // LANES, window, 0)

    m = mod_ref[0]
    xn = x_ref[...] + m[5:6] * acc_ref[...]
    if final:
        y = _rms(xn, fw_ref[...])
        for gi, ref in enumerate(o_refs):

            @pl.when(g == gi)
            def _():
                ref[...] = y
    else:
        o_refs[0][...] = xn


def _combine(lay, starts, x, slot_t, mod, final_w, ye, tb, final):
    nt, d = x.shape
    ng, n, ne = slot_t.shape
    cap = ye.shape[2]
    nb = n // tb
    span = min(LANES, cap)
    if final:
        out_specs = [pl.BlockSpec((tb, d),
                                  lambda g, b, st, gi=gi: (jnp.where(g == gi, b, jnp.where(g > gi, nb - 1, 0)), 0))
                     for gi in range(ng)]
        out_shape = [jax.ShapeDtypeStruct((n, d), F32) for _ in range(ng)]
    else:
        out_specs = [pl.BlockSpec((tb, d), lambda g, b, st: (g * nb + b, 0))]
        out_shape = [jax.ShapeDtypeStruct((nt, d), F32)]
    grid_spec = pltpu.PrefetchScalarGridSpec(
        num_scalar_prefetch=1,
        grid=(ng, nb),
        in_specs=[pl.BlockSpec((tb, d), lambda g, b, st: (g * nb + b, 0)),
                  pl.BlockSpec((1, tb, ne), lambda g, b, st: (g, b, 0)),
                  pl.BlockSpec((1, N_MOD, d), lambda g, b, st: (lay.seq_index((g * nb + b) * tb), 0, 0)),
                  pl.BlockSpec((1, d), lambda g, b, st: (0, 0)),
                  pl.BlockSpec(memory_space=pl.ANY)],
        out_specs=out_specs,
        scratch_shapes=[pltpu.VMEM((2, ne, span, d), BF16), pltpu.SemaphoreType.DMA((2, ne)),
                        pltpu.VMEM((LANES, d), BF16), pltpu.SemaphoreType.DMA((1,)),
                        pltpu.VMEM((tb, d), F32)])
    return pl.pallas_call(
        functools.partial(_combine_body, nb=nb, tb=tb, ne=ne, span=span, cap=cap, final=final),
        grid_spec=grid_spec,
        out_shape=out_shape,
        compiler_params=_cparams("arbitrary", "arbitrary"),
        name="ec_combine",
    )(starts, x, slot_t, mod, final_w, ye)


def _block_starts(rank, tb, cap):
    ng, ne = rank.shape[:2]
    first = rank[:, :, ::tb // LANES, 0]
    return jnp.concatenate([first, jnp.full((ng, ne, 1), cap, jnp.int32)], axis=2).reshape(-1)


def kernel(x_prompt, x_sample, c_prompt, c_sample, norm1_w, norm2_w, w_mod, b_mod, w_in, b_in, mlstm_norm_w, na_rpb, conv_w, conv_b, lru_wa, lru_ba, lru_wx, lru_bx, lru_L, w_br_a, w_br_b, w_br_c, w_out, w_router, b_router, w_gate_e, w_up_e, w_down_e, final_norm_w):
    bp, tp, d = x_prompt.shape
    bs, ts, _ = x_sample.shape
    lay = Layout(bp, tp, bs, ts, d)
    assert lay.n_p == lay.n_s, "the two request groups are routed as equal-sized token sets"
    depth = w_in.shape[0]
    ne = w_router.shape[2]
    ng, n = 2, lay.n_p
    cap = EC_FACTOR * n // ne
    tm_in = min(1024, ts)
    tn_in = 3 * d // 2
    tm_merge = min(512, ts)
    tt_lru = min(512, ts)
    tb_gather = min(2048, n)
    tsub_gather = min(512, n)
    tb_comb = min(512, ts)
    rt = min(512, cap)

    x = jnp.concatenate([x_prompt.reshape(lay.n_p, d), x_sample.reshape(lay.n_s, d)], axis=0)
    nseq = bp + bs
    rows_c = -(-nseq // SUBLANES) * SUBLANES
    c_all = jnp.zeros((rows_c, d), F32).at[:nseq].set(jnp.concatenate([c_prompt, c_sample], axis=0))
    mod_all = _modulation(c_all, w_mod, b_mod).reshape(depth, rows_c, N_MOD, d)

    widths = (d, d, d, d, 4 * MLSTM_HEADS, d, d, d, d, d, N_BRANCH * d)
    offs = [0]
    for w in widths:
        offs.append(offs[-1] + w)
    col = lambda a, i: a[..., offs[i]:offs[i + 1]]
    bcols = (0, 1, 2, 5, 6, 7)
    fcols = (3, 8, 9, 10)

    for l in range(depth):
        wl, bl = w_in[l], b_in[l]
        wb = jnp.concatenate([col(wl, i) for i in bcols], axis=1).astype(BF16)
        bb = jnp.concatenate([col(bl, i) for i in bcols], axis=0).reshape(1, -1)
        wf = jnp.concatenate([col(wl, i) for i in fcols], axis=1).astype(BF16)
        bf = jnp.concatenate([col(bl, i) for i in fcols], axis=0).reshape(1, -1)
        ngate = 4 * MLSTM_HEADS
        wg = jnp.zeros((d, LANES), BF16).at[:, :ngate].set(col(wl, 4).astype(BF16))
        bg = jnp.zeros((1, LANES), F32).at[0, :ngate].set(col(bl, 4))
        mod = mod_all[l]

        proj_b, proj_f, gates = _input_projection(lay, x, mod, norm1_w[l].reshape(1, d), wb, bb, wf, bf, wg, bg, tm_in,
                                                   tn_in)
        gates_t = gates[:, :ngate].T

        h_fwd, h_rev = _mlstm(lay, proj_b, gates, gates_t)

        bias = _na_bias_table(na_rpb[l])
        hb_p = _neigh_attn(proj_b, bias, d, 0, bp, tp, 3)
        hb_s = _neigh_attn(proj_b, bias, d, lay.n_p, bs, ts, 3)

        lru_args = (conv_w[l], conv_b[l].reshape(1, d))
        l_fwd = _lru_pass(lay, proj_f, *lru_args, _lru_gate_weights(lru_wa[l, 0], lru_wx[l, 0]),
                          lru_ba[l, 0].reshape(1, d), lru_bx[l, 0].reshape(1, d), lru_L[l, 0].reshape(1, d),
                          None, rev=False, tt=tt_lru)
        h_c = _lru_pass(lay, proj_f, *lru_args, _lru_gate_weights(lru_wa[l, 1], lru_wx[l, 1]),
                        lru_ba[l, 1].reshape(1, d), lru_bx[l, 1].reshape(1, d), lru_L[l, 1].reshape(1, d),
                        l_fwd, rev=True, tt=tt_lru)

        x, xm2, logits_t = _merge_project(
            lay, h_fwd, h_rev, mlstm_norm_w[l].reshape(1, d), hb_p, hb_s, h_c, proj_f, x, mod,
            w_br_a[l].astype(BF16), w_br_b[l].astype(BF16),
            w_br_c[l].astype(BF16), w_out[l].astype(BF16), norm2_w[l].reshape(1, d),
            w_router[l].T.astype(BF16), b_router[l].reshape(ne, 1), tm_merge)

        slot, rank, aff = _route(logits_t, ng, cap)
        ye = _expert_ffn(_block_starts(rank, tsub_gather, cap), xm2.reshape(ng, n, d), slot, aff,
                         w_gate_e[l], w_up_e[l], w_down_e[l],
                         cap, tb_gather, tsub_gather, rt)
        slot_t = slot.reshape(ng, ne, n).transpose(0, 2, 1)
        outs = _combine(lay, _block_starts(rank, tb_comb, cap), x, slot_t, mod, final_norm_w.reshape(1, d), ye,
                        tb_comb, final=(l == depth - 1))
        x = outs[0]

    return (outs[0].reshape(bp, tp, d), outs[1].reshape(bs, ts, d))
```
